```python
import math
import jax, jax.numpy as jnp
from jax import lax
import numpy as np

D_MODEL = 1024
BATCH = 8
SEQ = 2048
DEPTH = 2
DEC_BATCH = 128
DEC_SEQ = 8
PAST_LEN = 2048
PAGE_SIZE = 128

HG_HEADS = 8
HG_DK = 128
HG_DV = D_MODEL // HG_HEADS
HG_WIDTH = HG_HEADS * HG_DK
HG_VWIDTH = HG_HEADS * HG_DV
HG_COLS = 2 * HG_WIDTH + 2 * HG_VWIDTH
HG_CHUNK = 64
RW_HEAD = 64
RW_HEADS = D_MODEL // RW_HEAD
RW_WIDTH = RW_HEADS * RW_HEAD
RW_LORA_W = 64
RW_LORA_A = 64
RW_LORA_G = 128
RW_NCOL = 3 * RW_WIDTH + RW_LORA_W + RW_LORA_A + RW_LORA_G
RW_LNX_EPS = 64e-5
ATT_GROUPS = ((128, 1), (512, 4), (2048, 16))
ATT_HEADS = 8
ATT_HD = 64
ATT_WIDTH = ATT_HEADS * ATT_HD
ATT_COLS = 3 * len(ATT_GROUPS) * ATT_WIDTH
N_BRANCH = 3
GATE_COLS = N_BRANCH * D_MODEL
N_COLS = HG_COLS + RW_NCOL + ATT_COLS + GATE_COLS
D_FF = 2816
RMS_EPS = 1e-6

kernel_name = 'hgrn2_rwkv7_dilated_attn_macaron_step'


def rmsnorm(x, g):
    xf = x.astype(jnp.float32)
    y = xf * lax.rsqrt(jnp.mean(xf * xf, axis=-1, keepdims=True) + RMS_EPS)
    return (y * g.astype(jnp.float32)).astype(x.dtype)


def swiglu(x, w_gu, w_down):
    gate, up = jnp.split(x @ w_gu, 2, axis=-1)
    return (jax.nn.silu(gate) * up) @ w_down


def split_cols(a, widths):
    outs, off = [], 0
    for w in widths:
        outs.append(a[..., off:off + w])
        off += w
    return outs


def hgrn_lower_bounds(lb_param):
    p = jax.nn.softmax(lb_param.astype(jnp.float32), axis=0)
    c = jnp.cumsum(p, axis=0)
    return c - c[0:1]


def gla_chunked(q, k, v, log_f, s0):
    B, T, H, DK = q.shape
    DV = v.shape[-1]
    C = min(HG_CHUNK, T)
    n = -(-T // C)
    pad = n * C - T
    if pad:
        pw = ((0, 0), (0, pad), (0, 0), (0, 0))
        q, k, v, log_f = [jnp.pad(a, pw) for a in (q, k, v, log_f)]

    def to_chunks(a):
        return a.reshape(B, n, C, H, a.shape[-1]).transpose(1, 0, 3, 2, 4).astype(jnp.float32)

    qc, kc, vc, gc = [to_chunks(a) for a in (q, k, v, log_f)]
    causal = jnp.tril(jnp.ones((C, C), dtype=bool))

    def step(S, inp):
        qb, kb, vb, gb = inp
        G = jnp.cumsum(gb, axis=2)
        o_inter = jnp.einsum('bhtk,bhkv->bhtv', qb * jnp.exp(G), S)
        diff = G[:, :, :, None, :] - G[:, :, None, :, :]
        decay = jnp.exp(jnp.where(causal[:, :, None], diff, -jnp.inf))
        A = jnp.einsum('bhtk,bhsk,bhtsk->bhts', qb, kb, decay)
        o = o_inter + jnp.einsum('bhts,bhsv->bhtv', A, vb)
        G_last = G[:, :, -1:, :]
        S_new = jnp.exp(G_last[:, :, 0, :])[..., None] * S + jnp.einsum('bhsk,bhsv->bhkv', kb * jnp.exp(G_last - G), vb)
        return S_new, o

    S_fin, o = lax.scan(step, s0.astype(jnp.float32), (qc, kc, vc, gc))
    o = o.transpose(1, 0, 3, 2, 4).reshape(B, n * C, H, DV)[:, :T]
    return o, S_fin


def hgrn2_branch(cols, lb, gnorm, s0):
    B, T, _ = cols.shape
    q, f, i, og = split_cols(cols, (HG_WIDTH, HG_WIDTH, HG_VWIDTH, HG_VWIDTH))
    ff = f.astype(jnp.float32)
    log_f = jnp.logaddexp(jnp.log1p(-lb) + jax.nn.log_sigmoid(ff), jnp.log(lb))
    k = (1.0 - lb) * jax.nn.sigmoid(-ff)
    o, S = gla_chunked(jax.nn.silu(q).reshape(B, T, HG_HEADS, HG_DK),
                       k.reshape(B, T, HG_HEADS, HG_DK),
                       i.reshape(B, T, HG_HEADS, HG_DV),
                       log_f.reshape(B, T, HG_HEADS, HG_DK), s0)
    o = rmsnorm(o, gnorm.reshape(HG_HEADS, HG_DV)).reshape(B, T, HG_VWIDTH)
    return (o * jax.nn.silu(og.astype(jnp.float32))).astype(cols.dtype), S


def rwkv7_step(S, inp):
    r_t, w_t, k_t, v_t, kk_t, a_t = inp
    sa = jnp.einsum('bhvk,bhk->bhv', S, -kk_t)
    S = S * w_t[:, :, None, :] + sa[..., None] * (kk_t * a_t)[:, :, None, :] + v_t[..., None] * k_t[:, :, None, :]
    return S, jnp.einsum('bhvk,bhk->bhv', S, r_t)


def rwkv7_branch(cols, shift0, s0, lp):
    B, T, _ = cols.shape
    f32 = jnp.float32
    prev = jnp.concatenate([shift0[:, None].astype(cols.dtype), cols[:, :-1]], axis=1)
    xs = cols + lp['rw_mu'] * (prev - cols)
    r, k, v, wl, al, gl = split_cols(xs, (RW_WIDTH, RW_WIDTH, RW_WIDTH, RW_LORA_W, RW_LORA_A, RW_LORA_G))
    log_w = -math.exp(-0.5) * jax.nn.sigmoid((lp['rw_w0'] + jnp.tanh(wl) @ lp['rw_w_up']).astype(f32))
    a = jax.nn.sigmoid((lp['rw_a0'] + al @ lp['rw_a_up']).astype(f32))
    g = (jax.nn.sigmoid(gl) @ lp['rw_g_up']).astype(f32)

    def heads(t):
        return t.astype(f32).reshape(B, T, RW_HEADS, RW_HEAD)

    def hpar(p):
        return p.astype(f32).reshape(RW_HEADS, RW_HEAD)

    r, k, v, a, log_w = heads(r), heads(k), heads(v), heads(a), heads(log_w)
    kk = k * hpar(lp['rw_k_k'])
    kk = kk / jnp.maximum(jnp.sqrt(jnp.sum(kk * kk, axis=-1, keepdims=True)), 1e-12)
    k = k * (1.0 + (a - 1.0) * hpar(lp['rw_k_a']))

    def tm(t):
        return jnp.moveaxis(t, 1, 0)

    S_fin, y = lax.scan(rwkv7_step, s0.astype(f32), (tm(r), tm(jnp.exp(log_w)), tm(k), tm(v), tm(kk), tm(a)))
    y = jnp.moveaxis(y, 0, 1)
    mean = jnp.mean(y, axis=-1, keepdims=True)
    var = jnp.mean(jnp.square(y - mean), axis=-1, keepdims=True)
    y = (y - mean) * lax.rsqrt(var + RW_LNX_EPS) * hpar(lp['rw_lnx_w']) + hpar(lp['rw_lnx_b'])
    bonus = jnp.sum(r * k * lp['rw_r_k'].astype(f32), axis=-1, keepdims=True) * v
    out = (y + bonus).reshape(B, T, RW_WIDTH) * g
    return out.astype(cols.dtype), S_fin, cols[:, -1]


def dilated_block_prompt(q, k, v, dil, sub):
    B, S, H, D = q.shape
    ns = S // dil
    nb = -(-ns // sub)
    npad = nb * sub

    def by_residue(a):
        return a.reshape(B, ns, dil, H, D).transpose(0, 2, 1, 3, 4)

    kpad = ((0, 0), (0, 0), (sub, npad - ns), (0, 0), (0, 0))

    def key_blocks(a):
        a = jnp.pad(by_residue(a), kpad).reshape(B, dil, nb + 1, sub, H, D)
        return jnp.concatenate([a[:, :, :-1], a[:, :, 1:]], axis=3)

    qb = jnp.pad(by_residue(q), ((0, 0), (0, 0), (0, npad - ns), (0, 0), (0, 0))).reshape(B, dil, nb, sub, H, D)
    kb, vb = key_blocks(k), key_blocks(v)
    s = jnp.einsum('brnqhd,brnkhd->brnhqk', qb, kb).astype(jnp.float32) * (D ** -0.5)
    qi = jnp.arange(sub)[:, None]
    ki = jnp.arange(2 * sub)[None, :] - sub
    key_pos = jnp.arange(nb)[:, None, None] * sub + ki[None]
    dist = qi - ki
    valid = (dist >= 0) & (dist <= sub) & (key_pos >= 0)
    s = jnp.where(valid[None, None, :, None], s, -jnp.inf)
    m = jnp.max(s, axis=-1, keepdims=True)
    p = jnp.exp(s - m)
    den = jnp.sum(p, axis=-1)
    o = jnp.einsum('brnhqk,brnkhd->brnqhd', p, vb.astype(jnp.float32)) / jnp.swapaxes(den, -1, -2)[..., None]
    lse = jnp.swapaxes(m[..., 0] + jnp.log(den), -1, -2)
    o = o.reshape(B, dil, npad, H, D)[:, :, :ns].transpose(0, 2, 1, 3, 4).reshape(B, S, H, D)
    lse = lse.reshape(B, dil, npad, H)[:, :, :ns].transpose(0, 2, 1, 3).reshape(B, S, H)
    return o, lse


def dilated_gather_sample(q, kv_new, kv_cache, dil, sub):
    B, T, H, D = q.shape
    L = kv_cache.shape[1]
    kv = jnp.concatenate([kv_cache.astype(kv_new.dtype), kv_new], axis=1)
    idx = L + jnp.arange(T)[:, None] - jnp.arange(sub + 1)[None, :] * dil
    valid = idx >= 0
    g = jnp.take(kv, jnp.maximum(idx, 0), axis=1)
    s = jnp.einsum('bthd,btmhd->bthm', q, g[:, :, :, 0]).astype(jnp.float32) * (D ** -0.5)
    s = jnp.where(valid[None, :, None, :], s, -jnp.inf)
    m = jnp.max(s, axis=-1, keepdims=True)
    p = jnp.exp(s - m)
    den = jnp.sum(p, axis=-1)
    o = jnp.einsum('bthm,btmhd->bthd', p, g[:, :, :, 1].astype(jnp.float32)) / den[..., None]
    return o, m[..., 0] + jnp.log(den)


def dilated_attention_branch(c_att, caches):
    B, T, _ = c_att.shape
    per = split_cols(c_att, (ATT_WIDTH,) * (3 * len(ATT_GROUPS)))
    outs, lses, rows = [], [], []
    for gi, (win, dil) in enumerate(ATT_GROUPS):
        sub = win // dil
        q, k, v = [a.reshape(B, T, ATT_HEADS, ATT_HD) for a in per[3 * gi:3 * gi + 3]]
        kv = jnp.stack([k, v], axis=2)
        if caches is None:
            o, lse = dilated_block_prompt(q, k, v, dil, sub)
            rows.append(kv[:, T - min(win, T):])
        else:
            o, lse = dilated_gather_sample(q, kv, caches[gi], dil, sub)
            rows.append(kv)
        outs.append(o)
        lses.append(lse)
    wts = jax.nn.softmax(jnp.stack(lses, axis=0), axis=0)
    o = wts[0][..., None] * outs[0]
    for gi in range(1, len(ATT_GROUPS)):
        o = o + wts[gi][..., None] * outs[gi]
    return o.reshape(B, T, ATT_WIDTH).astype(c_att.dtype), rows


def trunk_layer(x, lp, lb, state):
    B, T, _ = x.shape
    f32 = jnp.float32
    x = x + 0.5 * swiglu(rmsnorm(x, lp['norm_ffn1']), lp['ffn1_w_gu'], lp['ffn1_w_down'])
    h = rmsnorm(x, lp['norm_mix'])
    cols = h @ lp['w_in']
    c_hg, c_rw, c_att, c_gate = split_cols(cols, (HG_COLS, RW_NCOL, ATT_COLS, GATE_COLS))
    if state is None:
        s_hg0 = jnp.zeros((B, HG_HEADS, HG_DK, HG_DV), f32)
        s_rw0 = jnp.zeros((B, RW_HEADS, RW_HEAD, RW_HEAD), f32)
        shift0 = jnp.zeros((B, RW_NCOL), cols.dtype)
        caches = None
    else:
        s_hg0, s_rw0, shift0, caches = state
    o_hg, s_hg = hgrn2_branch(c_hg, lb, lp['hg_gnorm'], s_hg0)
    o_rw, s_rw, shift = rwkv7_branch(c_rw, shift0, s_rw0, lp)
    o_att, kv_rows = dilated_attention_branch(c_att, caches)
    gates = jax.nn.sigmoid(c_gate).reshape(B, T, N_BRANCH, D_MODEL)
    merged = (gates[:, :, 0] * (o_hg @ lp['w_branch_hg'])
              + gates[:, :, 1] * (o_rw @ lp['w_branch_rw'])
              + gates[:, :, 2] * (o_att @ lp['w_branch_att']))
    x = x + merged @ lp['w_out']
    x = x + 0.5 * swiglu(rmsnorm(x, lp['norm_ffn2']), lp['ffn2_w_gu'], lp['ffn2_w_down'])
    return x, (s_hg, s_rw, shift, kv_rows[0], kv_rows[1], kv_rows[2])


def setup_inputs(seed: int = 0) -> dict:
    key = jax.random.key(seed)
    keys = jax.random.split(key, 64)
    counter = iter(range(64))
    f32 = jnp.float32

    def nrm(shape, scale=1.0):
        return scale * jax.random.normal(keys[next(counter)], shape, f32)

    def gain(shape):
        return 1.0 + 0.01 * nrm(shape)

    return {
        'x_prompt': nrm((BATCH, SEQ, D_MODEL)),
        'x_sample': nrm((DEC_BATCH, DEC_SEQ, D_MODEL)),
        'state_hgrn': nrm((DEPTH, DEC_BATCH, HG_HEADS, HG_DK, HG_DV), 0.5),
        'state_rwkv': nrm((DEPTH, DEC_BATCH, RW_HEADS, RW_HEAD, RW_HEAD), 0.5),
        'state_rwkv_shift': nrm((DEPTH, DEC_BATCH, RW_NCOL)),
        'cache_att1_kv': nrm((DEPTH, DEC_BATCH, min(ATT_GROUPS[0][0], PAST_LEN), 2, ATT_HEADS, ATT_HD)),
        'cache_att2_kv': nrm((DEPTH, DEC_BATCH, min(ATT_GROUPS[1][0], PAST_LEN), 2, ATT_HEADS, ATT_HD)),
        'cache_att3_kv': nrm((DEPTH, DEC_BATCH, min(ATT_GROUPS[2][0], PAST_LEN), 2, ATT_HEADS, ATT_HD)),
        'norm_ffn1': gain((DEPTH, D_MODEL)),
        'ffn1_w_gu': nrm((DEPTH, D_MODEL, 2 * D_FF), D_MODEL ** -0.5),
        'ffn1_w_down': nrm((DEPTH, D_FF, D_MODEL), D_FF ** -0.5),
        'norm_mix': gain((DEPTH, D_MODEL)),
        'w_in': nrm((DEPTH, D_MODEL, N_COLS), D_MODEL ** -0.5),
        'hg_lb': nrm((DEPTH, HG_WIDTH), 0.1),
        'hg_gnorm': gain((DEPTH, HG_VWIDTH)),
        'rw_mu': jax.random.uniform(keys[next(counter)], (DEPTH, RW_NCOL), f32),
        'rw_w0': nrm((DEPTH, RW_WIDTH), 0.5),
        'rw_w_up': nrm((DEPTH, RW_LORA_W, RW_WIDTH), 0.5 * RW_LORA_W ** -0.5),
        'rw_a0': nrm((DEPTH, RW_WIDTH), 0.5),
        'rw_a_up': nrm((DEPTH, RW_LORA_A, RW_WIDTH), 0.5 * RW_LORA_A ** -0.5),
        'rw_g_up': nrm((DEPTH, RW_LORA_G, RW_WIDTH), RW_LORA_G ** -0.5),
        'rw_k_k': 0.85 + 0.05 * nrm((DEPTH, RW_WIDTH)),
        'rw_k_a': 1.0 + 0.05 * nrm((DEPTH, RW_WIDTH)),
        'rw_r_k': nrm((DEPTH, RW_HEADS, RW_HEAD), 0.1),
        'rw_lnx_w': gain((DEPTH, RW_WIDTH)),
        'rw_lnx_b': nrm((DEPTH, RW_WIDTH), 0.01),
        'w_branch_hg': nrm((DEPTH, HG_VWIDTH, D_MODEL), HG_VWIDTH ** -0.5),
        'w_branch_rw': nrm((DEPTH, RW_WIDTH, D_MODEL), RW_WIDTH ** -0.5),
        'w_branch_att': nrm((DEPTH, ATT_WIDTH, D_MODEL), ATT_WIDTH ** -0.5),
        'w_out': nrm((DEPTH, D_MODEL, D_MODEL), D_MODEL ** -0.5),
        'norm_ffn2': gain((DEPTH, D_MODEL)),
        'ffn2_w_gu': nrm((DEPTH, D_MODEL, 2 * D_FF), D_MODEL ** -0.5),
        'ffn2_w_down': nrm((DEPTH, D_FF, D_MODEL), D_FF ** -0.5),
        'norm_final': gain((D_MODEL,)),
    }


def reference(x_prompt, x_sample, state_hgrn, state_rwkv, state_rwkv_shift, cache_att1_kv, cache_att2_kv, cache_att3_kv,
              norm_ffn1, ffn1_w_gu, ffn1_w_down, norm_mix, w_in, hg_lb, hg_gnorm,
              rw_mu, rw_w0, rw_w_up, rw_a0, rw_a_up, rw_g_up, rw_k_k, rw_k_a, rw_r_k, rw_lnx_w, rw_lnx_b,
              w_branch_hg, w_branch_rw, w_branch_att, w_out, norm_ffn2, ffn2_w_gu, ffn2_w_down, norm_final):
    lbs = hgrn_lower_bounds(hg_lb)
    yp, ys = x_prompt, x_sample
    p_states, s_states = [], []
    for l in range(DEPTH):
        lp = {
            'norm_ffn1': norm_ffn1[l], 'ffn1_w_gu': ffn1_w_gu[l], 'ffn1_w_down': ffn1_w_down[l],
            'norm_mix': norm_mix[l], 'w_in': w_in[l], 'hg_gnorm': hg_gnorm[l],
            'rw_mu': rw_mu[l], 'rw_w0': rw_w0[l], 'rw_w_up': rw_w_up[l], 'rw_a0': rw_a0[l],
            'rw_a_up': rw_a_up[l], 'rw_g_up': rw_g_up[l], 'rw_k_k': rw_k_k[l], 'rw_k_a': rw_k_a[l],
            'rw_r_k': rw_r_k[l], 'rw_lnx_w': rw_lnx_w[l], 'rw_lnx_b': rw_lnx_b[l],
            'w_branch_hg': w_branch_hg[l], 'w_branch_rw': w_branch_rw[l], 'w_branch_att': w_branch_att[l],
            'w_out': w_out[l], 'norm_ffn2': norm_ffn2[l], 'ffn2_w_gu': ffn2_w_gu[l], 'ffn2_w_down': ffn2_w_down[l],
        }
        yp, sp = trunk_layer(yp, lp, lbs[l], None)
        ys, ss = trunk_layer(ys, lp, lbs[l], (state_hgrn[l], state_rwkv[l], state_rwkv_shift[l],
                                               (cache_att1_kv[l], cache_att2_kv[l], cache_att3_kv[l])))
        p_states.append(sp)
        s_states.append(ss)
    p_hgrn, p_rwkv, p_shift, p_att1, p_att2, p_att3 = [jnp.stack(z, axis=0) for z in zip(*p_states)]
    s_hgrn, s_rwkv, s_shift, s_att1, s_att2, s_att3 = [jnp.stack(z, axis=0) for z in zip(*s_states)]
    y_prompt = rmsnorm(yp, norm_final)
    y_sample = rmsnorm(ys, norm_final)
    return (y_prompt, y_sample, p_hgrn, p_rwkv, p_shift, p_att1, p_att2, p_att3,
            s_hgrn, s_rwkv, s_shift, s_att1, s_att2, s_att3)
```

```python
import functools
import math

import numpy as np
import jax
import jax.numpy as jnp
from jax import lax
from jax.experimental import pallas as pl
from jax.experimental.pallas import tpu as pltpu

F32 = jnp.float32
BF16 = jnp.bfloat16

HG_HEADS = 8
HG_DK = 128
RW_HEAD = 64
RW_LORA_W = 64
RW_LORA_A = 64
RW_LORA_G = 128
RW_LNX_EPS = 64e-5
ATT_GROUPS = ((128, 1), (512, 4), (2048, 16))
ATT_HEADS = 8
ATT_HD = 64
ATT_WIDTH = ATT_HEADS * ATT_HD
RMS_EPS = 1e-6
LANES = 128
CHUNK = 64
VMEM_LIMIT = 48 * 1024 * 1024

NN = (((1,), (0,)), ((), ()))
NT = (((1,), (1,)), ((), ()))
TN = (((0,), (0,)), ((), ()))


def _dg(a, b, dims=NN):
    return lax.dot_general(a, b, dims, preferred_element_type=F32)


def _split2(x):
    hi = x.astype(BF16)
    lo = (x - hi.astype(F32)).astype(BF16)
    return hi, lo


def _split3(x):
    hi = x.astype(BF16)
    r = x - hi.astype(F32)
    mid = r.astype(BF16)
    lo = (r - mid.astype(F32)).astype(BF16)
    return hi, mid, lo


def _dot_sel(w, x, dims=NN):
    hi, mid, lo = _split3(x)
    return _dg(w, hi, dims) + _dg(w, mid, dims) + _dg(w, lo, dims)


def _dot_sel_rhs(x, w, dims=NN):
    hi, mid, lo = _split3(x)
    return _dg(hi, w, dims) + _dg(mid, w, dims) + _dg(lo, w, dims)


def _dot3(a, b, dims=NN):
    ah, al = _split2(a)
    bh, bl = _split2(b)
    return _dg(ah, bh, dims) + _dg(ah, bl, dims) + _dg(al, bh, dims)


def _rms(x, g):
    return x * lax.rsqrt(jnp.mean(x * x, axis=-1, keepdims=True) + RMS_EPS) * g


def _sigmoid(x):
    return 1.0 / (1.0 + jnp.exp(-x))


def _params(*sem):
    return pltpu.CompilerParams(dimension_semantics=sem, vmem_limit_bytes=VMEM_LIMIT)


def _row_tile(rows, want):
    t = min(rows, want)
    assert rows % t == 0
    return t


def _rmsnorm_body(x_ref, g_ref, o_ref):
    o_ref[...] = _rms(x_ref[...], g_ref[...]).astype(o_ref.dtype)


def rmsnorm_rows(x, g, out_dtype):
    rows, d = x.shape
    tm = _row_tile(rows, 1024)
    return pl.pallas_call(
        _rmsnorm_body,
        grid=(rows // tm,),
        in_specs=[pl.BlockSpec((tm, d), lambda i: (i, 0)), pl.BlockSpec((1, d), lambda i: (0, 0))],
        out_specs=pl.BlockSpec((tm, d), lambda i: (i, 0)),
        out_shape=jax.ShapeDtypeStruct((rows, d), out_dtype),
        compiler_params=_params("parallel"),
        name="rmsnorm",
    )(x, g.reshape(1, d))


def _ffn_body(x_ref, h_ref, wg_ref, wu_ref, wd_ref, gn_ref, xo_ref, ho_ref, acc_ref):
    j = pl.program_id(1)

    @pl.when(j == 0)
    def _():
        acc_ref[...] = jnp.zeros_like(acc_ref)

    h = h_ref[...]
    gate = jnp.dot(h, wg_ref[...], preferred_element_type=F32)
    up = jnp.dot(h, wu_ref[...], preferred_element_type=F32)
    act = (gate * _sigmoid(gate) * up).astype(BF16)
    acc_ref[...] += jnp.dot(act, wd_ref[...], preferred_element_type=F32)

    @pl.when(j == pl.num_programs(1) - 1)
    def _():
        xn = x_ref[...] + 0.5 * acc_ref[...]
        xo_ref[...] = xn
        ho_ref[...] = _rms(xn, gn_ref[...]).astype(ho_ref.dtype)


def ffn_half_step(x, h, w_gu, w_down, g_next, next_dtype):
    rows, d = x.shape
    dff = w_down.shape[0]
    tm = _row_tile(rows, 1024)
    tf = 256
    nf = dff // tf
    assert dff % tf == 0
    return pl.pallas_call(
        _ffn_body,
        grid=(rows // tm, nf),
        in_specs=[
            pl.BlockSpec((tm, d), lambda i, j: (i, 0)),
            pl.BlockSpec((tm, d), lambda i, j: (i, 0)),
            pl.BlockSpec((d, tf), lambda i, j: (0, j)),
            pl.BlockSpec((d, tf), lambda i, j: (0, j + nf)),
            pl.BlockSpec((tf, d), lambda i, j: (j, 0)),
            pl.BlockSpec((1, d), lambda i, j: (0, 0)),
        ],
        out_specs=[pl.BlockSpec((tm, d), lambda i, j: (i, 0)), pl.BlockSpec((tm, d), lambda i, j: (i, 0))],
        out_shape=[jax.ShapeDtypeStruct((rows, d), F32), jax.ShapeDtypeStruct((rows, d), next_dtype)],
        scratch_shapes=[pltpu.VMEM((tm, d), F32)],
        compiler_params=_params("parallel", "arbitrary"),
        name="ffn_half_step",
    )(x, h, w_gu, w_gu, w_down, g_next.reshape(1, d))


def _proj_body(h_ref, w_ref, o_ref):
    o_ref[...] = jnp.dot(h_ref[...], w_ref[...], preferred_element_type=F32)


def project(h, w, tn):
    rows, d = h.shape
    n = w.shape[1]
    tm = _row_tile(rows, 1024)
    assert n % tn == 0
    return pl.pallas_call(
        _proj_body,
        grid=(rows // tm, n // tn),
        in_specs=[pl.BlockSpec((tm, d), lambda i, j: (i, 0)), pl.BlockSpec((d, tn), lambda i, j: (0, j))],
        out_specs=pl.BlockSpec((tm, tn), lambda i, j: (i, j)),
        out_shape=jax.ShapeDtypeStruct((rows, n), F32),
        compiler_params=_params("parallel", "parallel"),
        name="project",
    )(h, w)


def _level_constants(rows, seq):
    levels = int(math.log2(seq))
    assert 2 ** levels == seq and rows % seq == 0
    t = np.arange(rows)
    r = t[None, :]
    ws, ms = [], [np.eye(rows, dtype=bool)]
    for l in range(1, levels + 1):
        m = 2 ** l
        hm = m // 2
        par, pos = t // m, t % m
        sec = pos >= hm
        ref = par * m + hm - 1
        w = np.where(sec[:, None], (r > ref[:, None]) & (r <= t[:, None]), (r > t[:, None]) & (r <= ref[:, None]))
        ws.append(w)
        ms.append((par[:, None] == par[None, :]) & sec[:, None] & (~sec)[None, :])
    same = (t // seq)[:, None] == (t // seq)[None, :]
    ws.append(same & (r <= t[:, None]))
    ws.append(same & (r > t[:, None]))
    return np.concatenate(ws, 0).astype(np.float32), np.stack(ms).astype(np.float32)


def _hgrn_gates(f, loglb, log1mlb, onemlb):
    ls = jnp.minimum(f, 0.0) - jnp.log1p(jnp.exp(-jnp.abs(f)))
    a = log1mlb + ls
    log_f = jnp.maximum(a, loglb) + jnp.log1p(jnp.exp(-jnp.abs(a - loglb)))
    key = onemlb * _sigmoid(-f)
    return -log_f, key


def _hgrn_intra(q, k, e_lv, masks, nlev):
    a = _dg(q, k, NT) * masks[0]
    for l in range(1, nlev + 1):
        e = e_lv[l - 1]
        a = a + _dg(q * e, k * e, NT) * masks[l]
    return a


def _hgrn_prompt_body(q_ref, f_ref, i_ref, g_ref, lb_ref, w_ref, m_ref, o_ref, s_ref, st_ref, *, nlev):
    c = pl.program_id(1)
    rows = q_ref.shape[0]

    @pl.when(c == 0)
    def _():
        st_ref[...] = jnp.zeros_like(st_ref)

    nlf, key = _hgrn_gates(f_ref[...], lb_ref[0:1, :], lb_ref[1:2, :], lb_ref[2:3, :])
    dec = jnp.exp(-_dot_sel(w_ref[...], nlf))
    qraw = q_ref[...]
    qact = qraw * _sigmoid(qraw)
    val = i_ref[...]
    og = g_ref[...]
    masks = [m_ref[l] for l in range(nlev + 1)]
    for h in range(HG_HEADS):
        sl = slice(h * HG_DK, (h + 1) * HG_DK)
        q, k, v = qact[:, sl], key[:, sl], val[:, sl]
        e_lv = [dec[l * rows:(l + 1) * rows, sl] for l in range(nlev)]
        e_cum = dec[nlev * rows:(nlev + 1) * rows, sl]
        e_suf = dec[(nlev + 1) * rows:(nlev + 2) * rows, sl]
        a = _hgrn_intra(q, k, e_lv, masks, nlev)
        st = st_ref[h]
        o = _dg(q * e_cum, st, NT) + _dg(a, v, NN)
        st_new = st * e_cum[rows - 1:rows, :] + _dg(v, k * e_suf, TN)
        st_ref[h] = st_new
        y = _rms(o, lb_ref[3:4, sl])
        ogh = og[:, sl]
        o_ref[:, sl] = y * (ogh * _sigmoid(ogh))

    @pl.when(c == pl.num_programs(1) - 1)
    def _():
        for h in range(HG_HEADS):
            s_ref[0, h] = st_ref[h].T


def hgrn_prompt(c_hg, lbp, batch, seq):
    width = c_hg.shape[1] // 4
    rows = CHUNK
    nlev = int(math.log2(rows))
    wmat, masks = _level_constants(rows, rows)
    nc = seq // rows
    col = lambda j: pl.BlockSpec((rows, width), lambda b, c: (b * nc + c, j))
    return pl.pallas_call(
        functools.partial(_hgrn_prompt_body, nlev=nlev),
        grid=(batch, nc),
        in_specs=[col(0), col(1), col(2), col(3),
                  pl.BlockSpec(lbp.shape, lambda b, c: (0, 0)),
                  pl.BlockSpec(wmat.shape, lambda b, c: (0, 0)),
                  pl.BlockSpec(masks.shape, lambda b, c: (0, 0, 0))],
        out_specs=[pl.BlockSpec((rows, width), lambda b, c: (b * nc + c, 0)),
                   pl.BlockSpec((1, HG_HEADS, HG_DK, HG_DK), lambda b, c: (b, 0, 0, 0))],
        out_shape=[jax.ShapeDtypeStruct((batch * seq, width), F32),
                   jax.ShapeDtypeStruct((batch, HG_HEADS, HG_DK, HG_DK), F32)],
        scratch_shapes=[pltpu.VMEM((HG_HEADS, HG_DK, HG_DK), F32)],
        compiler_params=_params("parallel", "arbitrary"),
        name="hgrn_prompt",
    )(c_hg, c_hg, c_hg, c_hg, lbp, jnp.asarray(wmat, BF16), jnp.asarray(masks))


def _hgrn_sample_body(q_ref, f_ref, i_ref, g_ref, lb_ref, w_ref, m_ref, s0_ref, o_ref, s_ref, *, nlev, seq):
    rows = q_ref.shape[0]
    nseq = rows // seq
    nlf, key = _hgrn_gates(f_ref[...], lb_ref[0:1, :], lb_ref[1:2, :], lb_ref[2:3, :])
    dec = jnp.exp(-_dot_sel(w_ref[...], nlf))
    qraw = q_ref[...]
    q = qraw * _sigmoid(qraw)
    v = i_ref[...]
    og = g_ref[...]
    masks = [m_ref[l] for l in range(nlev + 1)]
    e_lv = [dec[l * rows:(l + 1) * rows] for l in range(nlev)]
    e_cum = dec[nlev * rows:(nlev + 1) * rows]
    e_suf = dec[(nlev + 1) * rows:(nlev + 2) * rows]
    a = _hgrn_intra(q, key, e_lv, masks, nlev)
    o_intra = _dg(a, v, NN)
    qc = q * e_cum
    ks = key * e_suf
    ones = jnp.ones((seq, HG_DK), BF16)
    outs = []
    for b in range(nseq):
        rs = slice(b * seq, (b + 1) * seq)
        s0 = s0_ref[b, 0]
        outs.append(_dg(qc[rs], s0, NN))
        total = _dot_sel_rhs(nlf[rs], ones, TN)
        s_ref[b, 0] = jnp.exp(-total) * s0 + _dg(ks[rs], v[rs], TN)
    o = o_intra + jnp.concatenate(outs, axis=0)
    y = _rms(o, lb_ref[3:4, :])
    o_ref[...] = y * (og * _sigmoid(og))


def hgrn_sample(c_hg, lbp, state, layer, batch, seq):
    width = c_hg.shape[1] // 4
    nh = width // HG_DK
    rows = CHUNK
    nseq = rows // seq
    nlev = int(math.log2(seq))
    wmat, masks = _level_constants(rows, seq)
    col = lambda j: pl.BlockSpec((rows, HG_DK), lambda i, h: (i, j * nh + h))
    lbspec = pl.BlockSpec((lbp.shape[0], HG_DK), lambda i, h: (0, h))
    return pl.pallas_call(
        functools.partial(_hgrn_sample_body, nlev=nlev, seq=seq),
        grid=(batch // nseq, nh),
        in_specs=[col(0), col(1), col(2), col(3), lbspec,
                  pl.BlockSpec(wmat.shape, lambda i, h: (0, 0)),
                  pl.BlockSpec(masks.shape, lambda i, h: (0, 0, 0)),
                  pl.BlockSpec((None, nseq, 1, HG_DK, HG_DK), lambda i, h: (layer, i, h, 0, 0))],
        out_specs=[pl.BlockSpec((rows, HG_DK), lambda i, h: (i, h)),
                   pl.BlockSpec((nseq, 1, HG_DK, HG_DK), lambda i, h: (i, h, 0, 0))],
        out_shape=[jax.ShapeDtypeStruct((batch * seq, width), F32),
                   jax.ShapeDtypeStruct((batch, nh, HG_DK, HG_DK), F32)],
        compiler_params=_params("parallel", "parallel"),
        name="hgrn_sample",
    )(c_hg, c_hg, c_hg, c_hg, lbp, jnp.asarray(wmat, BF16), jnp.asarray(masks), state)


def _group_sum(x, bd):
    hi, lo = _split2(x)
    return _dg(hi, bd, NN) + _dg(lo, bd, NN)


def _rwkv_prep_body(c_ref, p_ref, mu_ref, vec_ref, wup_ref, aup_ref, gup_ref, bd_ref,
                    r_ref, k_ref, v_ref, kk_ref, ab_ref, lw_ref, g_ref):
    c = c_ref[...]
    xs = c + mu_ref[...] * (p_ref[...] - c)
    w = r_ref.shape[1]
    r, k, v = xs[:, 0:w], xs[:, w:2 * w], xs[:, 2 * w:3 * w]
    lora = xs[:, 3 * w:3 * w + RW_LORA_W + RW_LORA_A]
    gl = xs[:, 3 * w + RW_LORA_W + RW_LORA_A:]
    w0, a0, k_k, k_a = vec_ref[0:1, :], vec_ref[1:2, :], vec_ref[2:3, :], vec_ref[3:4, :]
    lw = -math.exp(-0.5) * _sigmoid(w0 + _dg(jnp.tanh(lora), wup_ref[...]))
    a = _sigmoid(a0 + _dg(lora, aup_ref[...]))
    g_ref[...] = _dg(_sigmoid(gl), gup_ref[...])
    kkr = k * k_k
    sq = kkr * kkr
    bd = bd_ref[...]
    ss = jnp.concatenate([_group_sum(sq[:, j * LANES:(j + 1) * LANES], bd) for j in range(w // LANES)], axis=1)
    kk = kkr / jnp.maximum(jnp.sqrt(ss), 1e-12)
    r_ref[...] = r
    v_ref[...] = v
    k_ref[...] = k * (1.0 + (a - 1.0) * k_a)
    kk_ref[...] = kk
    ab_ref[...] = kk * a
    lw_ref[...] = lw


def _head_blockdiag(n, head):
    i = np.arange(n) // head
    return (i[:, None] == i[None, :]).astype(np.float32)


def rwkv_prep(c_rw, prev, mu, vecs, wup_pad, aup_pad, g_up):
    rows, ncol = c_rw.shape
    w = vecs.shape[1]
    tm = _row_tile(rows, 256)
    bd = jnp.asarray(_head_blockdiag(LANES, RW_HEAD), BF16)
    full = lambda a: pl.BlockSpec(a.shape, lambda i: (0,) * a.ndim)
    row = lambda n: pl.BlockSpec((tm, n), lambda i: (i, 0))
    mu2 = mu.reshape(1, ncol)
    return pl.pallas_call(
        _rwkv_prep_body,
        grid=(rows // tm,),
        in_specs=[row(ncol), row(ncol), full(mu2), full(vecs), full(wup_pad), full(aup_pad), full(g_up), full(bd)],
        out_specs=[row(w)] * 7,
        out_shape=[jax.ShapeDtypeStruct((rows, w), F32)] * 7,
        compiler_params=_params("parallel"),
        name="rwkv_prep",
    )(c_rw, prev, mu2, vecs, wup_pad, aup_pad, g_up, bd)


def _rwkv_consts(rows, seq):
    t = np.arange(rows)
    sq = t // seq
    same = sq[:, None] == sq[None, :]
    r = t[None, :]
    pre = same & (r <= t[:, None])
    mid = sq * seq + seq // 2 - 1
    premid = same & (r <= mid[:, None])
    suf = same & (r > t[:, None])
    wmat = np.concatenate([pre.astype(np.float32) - premid.astype(np.float32), pre, suf], 0)
    t2 = np.concatenate([t, t])
    h2 = np.concatenate([0 * t, 0 * t + 1])
    s2 = t2 // seq
    blk = (h2[:, None] == h2[None, :]) & (s2[:, None] == s2[None, :])
    strict = blk & (t2[None, :] < t2[:, None])
    incl = blk & (t2[None, :] <= t2[:, None])
    lv = []
    for l in range(1, int(math.log2(seq)) + 1):
        m = 2 ** l
        par, pos = t2 // m, t2 % m
        lv.append(blk & (par[:, None] == par[None, :]) & (pos[:, None] >= m // 2) & (pos[None, :] < m // 2))
    masks = np.stack([strict, incl, np.eye(2 * rows, dtype=bool)] + lv).astype(np.float32)
    return wmat, masks


def _stack2(x, m0, m1):
    return jnp.concatenate([x * m0, x * m1], axis=0)


def _unit_lower_inverse(n, masks, nlev):
    t = masks[2] + n * masks[3]
    for l in range(2, nlev + 1):
        t = t + _dot3(_dot3(t, n * masks[2 + l]), t)
    return t


def _rwkv_chunk_body(r_ref, k_ref, v_ref, kk_ref, ab_ref, lw_ref, w_ref, m_ref,
                     r2_ref, y2_ref, mx_ref, z_ref, *, seq):
    rows = r_ref.shape[0]
    nseq = rows // seq
    nlev = int(math.log2(seq))
    lw = lw_ref[...]
    d = _dot_sel(w_ref[...], lw)
    dm, gin, suf = d[0:rows], d[rows:2 * rows], d[2 * rows:3 * rows]
    e_mid = jnp.exp(dm)
    e_midx = jnp.exp(dm - lw)
    e_inv = jnp.exp(-dm)
    e_abs = jnp.exp(gin)
    e_absx = jnp.exp(gin - lw)
    e_suf = jnp.exp(suf)
    rr, kx, vv, bb = r_ref[...], k_ref[...], v_ref[...], ab_ref[...]
    aa = -kk_ref[...]
    lane = lax.broadcasted_iota(jnp.int32, (rows, LANES), 1)
    m0 = (lane < RW_HEAD).astype(F32)
    m1 = 1.0 - m0
    masks = [m_ref[i] for i in range(3 + nlev)]
    for p in range(r_ref.shape[1] // LANES):
        sl = slice(p * LANES, (p + 1) * LANES)
        s2 = lambda x: _stack2(x[:, sl], m0, m1)
        lhs = jnp.concatenate([s2(aa * e_midx), s2(rr * e_mid)], axis=0)
        rhs = jnp.concatenate([s2(bb * e_inv), s2(kx * e_inv)], axis=0)
        big = _dot3(lhs, rhs, NT)
        n2 = 2 * rows
        a_ab = big[0:n2, 0:n2] * masks[0]
        a_ak = big[0:n2, n2:] * masks[0]
        a_rb = big[n2:, 0:n2] * masks[1]
        a_rk = big[n2:, n2:] * masks[1]
        tinv = _unit_lower_inverse(a_ab, masks, nlev)
        v_st = s2(vv)
        p_st = _dot3(tinv, s2(aa * e_absx))
        q_st = _dot3(tinv, _dg(a_ak, v_st))
        r2 = s2(rr * e_abs) + _dg(a_rb, p_st)
        y2 = _dg(a_rb, q_st) + _dg(a_rk, v_st)
        r2_ref[:, sl] = r2[0:rows] + r2[rows:]
        y2_ref[:, sl] = y2[0:rows] + y2[rows:]
        b_end = bb * e_suf
        k_end = kx * e_suf
        for b in range(nseq):
            rs = slice(b * seq, (b + 1) * seq)
            pick = lambda x: jnp.concatenate([x[rs], x[rows + b * seq:rows + (b + 1) * seq]], axis=0)
            b_st = pick(s2(b_end))
            mx = _dg(pick(p_st), b_st, TN)
            zz = _dg(pick(q_st), b_st, TN) + _dg(pick(v_st), pick(s2(k_end)), TN)
            g_end = e_abs[b * seq + seq - 1:b * seq + seq, sl]
            eye = masks[2][0:RW_HEAD, 0:LANES] + masks[2][RW_HEAD:LANES, 0:LANES]
            mx_ref[b * RW_HEAD:(b + 1) * RW_HEAD, sl] = mx[0:RW_HEAD] + mx[RW_HEAD:] + eye * g_end
            z_ref[b * RW_HEAD:(b + 1) * RW_HEAD, sl] = zz[0:RW_HEAD] + zz[RW_HEAD:]


def rwkv_chunk(r, kmod, v, kk, ab, lw, seq):
    rows_all, w = r.shape
    rows = CHUNK
    seq = min(seq, rows)
    nseq = rows // seq
    wmat, masks = _rwkv_consts(rows, seq)
    row = pl.BlockSpec((rows, w), lambda i: (i, 0))
    st = pl.BlockSpec((nseq * RW_HEAD, w), lambda i: (i, 0))
    nst = rows_all // seq * RW_HEAD
    return pl.pallas_call(
        functools.partial(_rwkv_chunk_body, seq=seq),
        grid=(rows_all // rows,),
        in_specs=[row] * 6 + [pl.BlockSpec(wmat.shape, lambda i: (0, 0)), pl.BlockSpec(masks.shape, lambda i: (0, 0, 0))],
        out_specs=[row, row, st, st],
        out_shape=[jax.ShapeDtypeStruct((rows_all, w), F32)] * 2 + [jax.ShapeDtypeStruct((nst, w), F32)] * 2,
        compiler_params=_params("parallel"),
        name="rwkv_chunk",
    )(r, kmod, v, kk, ab, lw, jnp.asarray(wmat, BF16), jnp.asarray(masks))


def _rwkv_scan_body(r2_ref, y2_ref, mx_ref, z_ref, s0_ref, r_ref, k_ref, v_ref, g_ref, vec_ref, bd_ref,
                    o_ref, s_ref, st_ref, *, seq):
    c = pl.program_id(1)
    rows = r2_ref.shape[0]
    nseq = rows // seq
    w = r2_ref.shape[1]

    @pl.when(c == 0)
    def _():
        st_ref[...] = s0_ref[...]

    lane = lax.broadcasted_iota(jnp.int32, (RW_HEAD, LANES), 1)
    m0 = (lane < RW_HEAD).astype(F32)
    m1 = 1.0 - m0
    r2, y2 = r2_ref[...], y2_ref[...]
    ys = []
    for b in range(nseq):
        rs = slice(b * seq, (b + 1) * seq)
        hs = slice(b * RW_HEAD, (b + 1) * RW_HEAD)
        yb = []
        for p in range(w // LANES):
            sl = slice(p * LANES, (p + 1) * LANES)
            s = st_ref[b, :, sl]
            yb.append(_dg(r2[rs, sl], _stack2(s, m0, m1), NT) + y2[rs, sl])
            st_ref[b, :, sl] = _dg(s, _stack2(mx_ref[hs, sl], m0, m1), NN) + z_ref[hs, sl]
        ys.append(jnp.concatenate(yb, axis=1))
    y = jnp.concatenate(ys, axis=0) if nseq > 1 else ys[0]
    bd = bd_ref[...]
    gs = lambda x: jnp.concatenate([_group_sum(x[:, j * LANES:(j + 1) * LANES], bd) for j in range(w // LANES)], axis=1)
    mean = gs(y) * (1.0 / RW_HEAD)
    yc = y - mean
    var = gs(yc * yc) * (1.0 / RW_HEAD)
    yn = yc * lax.rsqrt(var + RW_LNX_EPS) * vec_ref[0:1, :] + vec_ref[1:2, :]
    bonus = gs(r_ref[...] * k_ref[...] * vec_ref[2:3, :]) * v_ref[...]
    o_ref[...] = (yn + bonus) * g_ref[...]

    @pl.when(c == pl.num_programs(1) - 1)
    def _():
        s_ref[...] = st_ref[...]


def rwkv_scan(r2, y2, mx, z, s0, r, kmod, v, g, vecs, batch, seq):
    rows_all, w = r2.shape
    rows = CHUNK
    cseq = min(seq, rows)
    nseq = rows // cseq
    nc = seq // cseq
    bd = jnp.asarray(_head_blockdiag(LANES, RW_HEAD), BF16)
    row = pl.BlockSpec((rows, w), lambda i, c: (i * nc + c, 0))
    st = pl.BlockSpec((nseq * RW_HEAD, w), lambda i, c: (i * nc + c, 0))
    sblk = pl.BlockSpec((nseq, RW_HEAD, w), lambda i, c: (i, 0, 0))
    return pl.pallas_call(
        functools.partial(_rwkv_scan_body, seq=cseq),
        grid=(batch // nseq, nc),
        in_specs=[row, row, st, st, sblk, row, row, row, row,
                  pl.BlockSpec(vecs.shape, lambda i, c: (0, 0)), pl.BlockSpec(bd.shape, lambda i, c: (0, 0))],
        out_specs=[row, sblk],
        out_shape=[jax.ShapeDtypeStruct((rows_all, w), F32), jax.ShapeDtypeStruct(s0.shape, F32)],
        scratch_shapes=[pltpu.VMEM((nseq, RW_HEAD, w), F32)],
        compiler_params=_params("parallel", "arbitrary"),
        name="rwkv_scan",
    )(r2, y2, mx, z, s0, r, kmod, v, g, vecs, bd)


def rwkv_branch(c_rw, shift0, s0, lp, batch, seq):
    ncol = c_rw.shape[1]
    w = lp['rw_w0'].shape[0]
    nh = w // RW_HEAD
    c3 = c_rw.reshape(batch, seq, ncol)
    prev = jnp.concatenate([shift0[:, None], c3[:, :-1]], axis=1).reshape(batch * seq, ncol)
    vecs = jnp.stack([lp['rw_w0'], lp['rw_a0'], lp['rw_k_k'], lp['rw_k_a']])
    zpad = lambda a, top: jnp.concatenate([a, jnp.zeros_like(a)] if top else [jnp.zeros_like(a), a], axis=0)
    r, kmod, v, kk, ab, lw, g = rwkv_prep(c_rw, prev, lp['rw_mu'], vecs, zpad(lp['rw_w_up'], True),
                                          zpad(lp['rw_a_up'], False), lp['rw_g_up'])
    r2, y2, mx, z = rwkv_chunk(r, kmod, v, kk, ab, lw, seq)
    s0t = jnp.transpose(s0, (0, 2, 1, 3)).reshape(batch, RW_HEAD, w)
    ovecs = jnp.stack([lp['rw_lnx_w'], lp['rw_lnx_b'], lp['rw_r_k'].reshape(w)])
    out, s_fin = rwkv_scan(r2, y2, mx, z, s0t, r, kmod, v, g, ovecs, batch, seq)
    s_fin = jnp.transpose(s_fin.reshape(batch, RW_HEAD, nh, RW_HEAD), (0, 2, 1, 3))
    return out, s_fin, c3[:, -1]


def _att_prompt_body(q_ref, kp_ref, kc_ref, vp_ref, vc_ref, o_ref, l_ref, *, sub):
    n = pl.program_id(2)
    scale = ATT_HD ** -0.5
    qi = lax.broadcasted_iota(jnp.int32, (sub, 2 * sub), 0)
    ki = lax.broadcasted_iota(jnp.int32, (sub, 2 * sub), 1) - sub
    dist = qi - ki
    valid = (dist >= 0) & (dist <= sub) & (n * sub + ki >= 0)
    lane = lax.broadcasted_iota(jnp.int32, (1, LANES), 1)
    hm = [(lane < ATT_HD).astype(F32), (lane >= ATT_HD).astype(F32)]
    for p in range(q_ref.shape[2] // LANES):
        sl = slice(p * LANES, (p + 1) * LANES)
        q = q_ref[0, :, sl]
        kcat = jnp.concatenate([kp_ref[0, :, sl], kc_ref[0, :, sl]], axis=0)
        vcat = jnp.concatenate([vp_ref[0, :, sl], vc_ref[0, :, sl]], axis=0)
        o = jnp.zeros((sub, LANES), F32)
        lse = jnp.zeros((sub, LANES), F32)
        for m in hm:
            s = _dg(q * m, kcat, NT) * scale
            s = jnp.where(valid, s, -jnp.inf)
            mx = jnp.max(s, axis=-1, keepdims=True)
            pr = jnp.exp(s - mx)
            den = jnp.sum(pr, axis=-1, keepdims=True)
            o = o + _dg(pr, vcat * m, NN) / den
            lse = lse + (mx + jnp.log(den)) * m
        o_ref[0, :, sl] = o
        l_ref[0, :, sl] = lse


def att_prompt_group(c_att, gi, dil, batch, seq):
    ncol = c_att.shape[1]
    sub = ATT_GROUPS[gi][0] // dil
    ns = seq // dil
    nb = ns // sub
    assert ns % sub == 0
    nblk = ncol // ATT_WIDTH
    cv = c_att.reshape(batch, ns, dil * ncol)
    cur = lambda j: pl.BlockSpec((1, sub, ATT_WIDTH), lambda b, r, n: (b, n, r * nblk + 3 * gi + j))
    prv = lambda j: pl.BlockSpec((1, sub, ATT_WIDTH), lambda b, r, n: (b, jnp.maximum(n - 1, 0), r * nblk + 3 * gi + j))
    outb = pl.BlockSpec((1, sub, ATT_WIDTH), lambda b, r, n: (b, n, r))
    o, lse = pl.pallas_call(
        functools.partial(_att_prompt_body, sub=sub),
        grid=(batch, dil, nb),
        in_specs=[cur(0), prv(1), cur(1), prv(2), cur(2)],
        out_specs=[outb, outb],
        out_shape=[jax.ShapeDtypeStruct((batch, ns, dil * ATT_WIDTH), F32)] * 2,
        compiler_params=_params("parallel", "parallel", "parallel"),
        name="att_prompt",
    )(cv, cv, cv, cv, cv)
    return o.reshape(batch * seq, ATT_WIDTH), lse.reshape(batch * seq, ATT_WIDTH)


def _att_sample_body(q_ref, kn_ref, vn_ref, c_ref, bd_ref, o_ref, l_ref, *, dil, sub):
    seq = q_ref.shape[0]
    scale = ATT_HD ** -0.5
    bd = bd_ref[...]
    q, kn, vn = q_ref[...], kn_ref[...], vn_ref[...]
    crow = lax.broadcasted_iota(jnp.int32, (sub, ATT_WIDTH), 0)
    nrow = lax.broadcasted_iota(jnp.int32, (seq, ATT_WIDTH), 0)
    outs, lses = [], []
    for t in range(seq):
        res = t % dil
        kc = c_ref[0, :, res * 2 * ATT_WIDTH:res * 2 * ATT_WIDTH + ATT_WIDTH]
        vc = c_ref[0, :, res * 2 * ATT_WIDTH + ATT_WIDTH:(res + 1) * 2 * ATT_WIDTH]
        qt = q[t:t + 1, :]
        sc = _group_sum(kc * qt, bd) * scale
        sc = jnp.where(crow >= t // dil, sc, -jnp.inf)
        sn = _group_sum(kn * qt, bd) * scale
        sn = jnp.where((nrow <= t) & ((t - nrow) % dil == 0), sn, -jnp.inf)
        mx = jnp.maximum(jnp.max(sc, axis=0, keepdims=True), jnp.max(sn, axis=0, keepdims=True))
        pc = jnp.exp(sc - mx)
        pn = jnp.exp(sn - mx)
        den = jnp.sum(pc, axis=0, keepdims=True) + jnp.sum(pn, axis=0, keepdims=True)
        num = jnp.sum(pc * vc, axis=0, keepdims=True) + jnp.sum(pn * vn, axis=0, keepdims=True)
        outs.append(num / den)
        lses.append(mx + jnp.log(den))
    o_ref[...] = jnp.concatenate(outs, axis=0)
    l_ref[...] = jnp.concatenate(lses, axis=0)


def att_sample_group(c_att, cache, layer, gi, batch, seq):
    win, dil = ATT_GROUPS[gi]
    sub = win // dil
    length = cache.shape[2]
    assert length == sub * dil and seq <= sub
    nres = min(dil, seq)
    cv = cache.reshape(cache.shape[0], batch, sub, dil * 2 * ATT_WIDTH)
    bd = jnp.asarray(_head_blockdiag(ATT_WIDTH, ATT_HD), BF16)
    new = lambda j: pl.BlockSpec((seq, ATT_WIDTH), lambda b: (b, 3 * gi + j))
    outb = pl.BlockSpec((seq, ATT_WIDTH), lambda b: (b, 0))
    return pl.pallas_call(
        functools.partial(_att_sample_body, dil=dil, sub=sub),
        grid=(batch,),
        in_specs=[new(0), new(1), new(2),
                  pl.BlockSpec((None, 1, sub, nres * 2 * ATT_WIDTH), lambda b: (layer, b, 0, 0)),
                  pl.BlockSpec(bd.shape, lambda b: (0, 0))],
        out_specs=[outb, outb],
        out_shape=[jax.ShapeDtypeStruct((batch * seq, ATT_WIDTH), F32)] * 2,
        compiler_params=_params("parallel"),
        name="att_sample",
    )(c_att, c_att, c_att, cv, bd)


def _merge_body(x_ref, gate_ref, hg_ref, rw_ref, o1_ref, o2_ref, o3_ref, l1_ref, l2_ref, l3_ref,
                whg_ref, wrw_ref, watt_ref, wout_ref, gn_ref, xo_ref, ho_ref):
    d = x_ref.shape[1]
    l1, l2, l3 = l1_ref[...], l2_ref[...], l3_ref[...]
    mx = jnp.maximum(jnp.maximum(l1, l2), l3)
    e1, e2, e3 = jnp.exp(l1 - mx), jnp.exp(l2 - mx), jnp.exp(l3 - mx)
    att = (e1 * o1_ref[...] + e2 * o2_ref[...] + e3 * o3_ref[...]) / (e1 + e2 + e3)
    gates = _sigmoid(gate_ref[...])
    bm = lambda a, w_ref: jnp.dot(a.astype(BF16), w_ref[...], preferred_element_type=F32)
    merged = (gates[:, 0:d] * bm(hg_ref[...], whg_ref)
              + gates[:, d:2 * d] * bm(rw_ref[...], wrw_ref)
              + gates[:, 2 * d:3 * d] * bm(att, watt_ref))
    xn = x_ref[...] + bm(merged, wout_ref)
    xo_ref[...] = xn
    ho_ref[...] = _rms(xn, gn_ref[...]).astype(ho_ref.dtype)


def merge_branches(x, c_gate, o_hg, o_rw, att_o, att_l, w_hg, w_rw, w_att, w_out, g_next):
    rows, d = x.shape
    tm = _row_tile(rows, 256)
    row = lambda a: pl.BlockSpec((tm, a.shape[1]), lambda i: (i, 0))
    full = lambda a: pl.BlockSpec(a.shape, lambda i: (0, 0))
    g2 = g_next.reshape(1, d)
    args = [x, c_gate, o_hg, o_rw, *att_o, *att_l]
    wts = [w_hg, w_rw, w_att, w_out, g2]
    return pl.pallas_call(
        _merge_body,
        grid=(rows // tm,),
        in_specs=[row(a) for a in args] + [full(a) for a in wts],
        out_specs=[pl.BlockSpec((tm, d), lambda i: (i, 0))] * 2,
        out_shape=[jax.ShapeDtypeStruct((rows, d), F32), jax.ShapeDtypeStruct((rows, d), BF16)],
        compiler_params=_params("parallel"),
        name="merge_branches",
    )(*args, *wts)


def _trunk_layer(x, h, lw, layer, batch, seq, state, last):
    d = x.shape[1]
    x, h = ffn_half_step(x, h, lw['ffn1_w_gu'], lw['ffn1_w_down'], lw['norm_mix'], BF16)
    c_hg = project(h, lw['w_in_hg'], 512)
    c_rw = project(h, lw['w_in_rw'], 256)
    c_att = project(h, lw['w_in_att'], 512)
    c_gate = project(h, lw['w_in_gate'], 512)
    if state is None:
        o_hg, s_hg = hgrn_prompt(c_hg, lw['hg_params'], batch, seq)
        shift0 = jnp.zeros((batch, c_rw.shape[1]), F32)
        s_rw0 = jnp.zeros((batch, d // RW_HEAD, RW_HEAD, RW_HEAD), F32)
    else:
        s_hgrn, s_rwkv, s_shift, caches = state
        o_hg, s_hg = hgrn_sample(c_hg, lw['hg_params'], s_hgrn, layer, batch, seq)
        shift0, s_rw0 = s_shift[layer], s_rwkv[layer]
    o_rw, s_rw, shift = rwkv_branch(c_rw, shift0, s_rw0, lw, batch, seq)
    att_o, att_l, kv_rows = [], [], []
    c_att3 = c_att.reshape(batch, seq, -1)
    for gi, (win, dil) in enumerate(ATT_GROUPS):
        if state is None:
            o, l = att_prompt_group(c_att, gi, dil, batch, seq)
        else:
            o, l = att_sample_group(c_att, caches[gi], layer, gi, batch, seq)
        att_o.append(o)
        att_l.append(l)
        keep = min(win, seq)
        kv = c_att3[:, seq - keep:, (3 * gi + 1) * ATT_WIDTH:(3 * gi + 3) * ATT_WIDTH]
        kv_rows.append(kv.reshape(batch, keep, 2, ATT_HEADS, ATT_HD))
    x, h = merge_branches(x, c_gate, o_hg, o_rw, att_o, att_l, lw['w_branch_hg'], lw['w_branch_rw'],
                          lw['w_branch_att'], lw['w_out'], lw['norm_ffn2'])
    x, h = ffn_half_step(x, h, lw['ffn2_w_gu'], lw['ffn2_w_down'], lw['norm_next'], F32 if last else BF16)
    return x, h, (s_hg, s_rw, shift, *kv_rows)


def _lower_bounds(hg_lb):
    p = jax.nn.softmax(hg_lb.astype(F32), axis=0)
    c = jnp.cumsum(p, axis=0)
    return c - c[0:1]


def kernel(x_prompt, x_sample, state_hgrn, state_rwkv, state_rwkv_shift, cache_att1_kv, cache_att2_kv, cache_att3_kv, norm_ffn1, ffn1_w_gu, ffn1_w_down, norm_mix, w_in, hg_lb, hg_gnorm, rw_mu, rw_w0, rw_w_up, rw_a0, rw_a_up, rw_g_up, rw_k_k, rw_k_a, rw_r_k, rw_lnx_w, rw_lnx_b, w_branch_hg, w_branch_rw, w_branch_att, w_out, norm_ffn2, ffn2_w_gu, ffn2_w_down, norm_final):
    depth = norm_ffn1.shape[0]
    bp, sp, d = x_prompt.shape
    bs, ss, _ = x_sample.shape
    hg_cols = 2 * HG_HEADS * HG_DK + 2 * hg_gnorm.shape[1]
    rw_cols = rw_mu.shape[1]
    att_cols = 3 * len(ATT_GROUPS) * ATT_WIDTH
    bounds = np.cumsum([0, hg_cols, rw_cols, att_cols, 3 * d])
    assert bounds[-1] == w_in.shape[2]
    lbs = _lower_bounds(hg_lb)
    layers = []
    for l in range(depth):
        lb = lbs[l]
        wl = w_in[l].astype(BF16)
        layers.append({
            'ffn1_w_gu': ffn1_w_gu[l].astype(BF16), 'ffn1_w_down': ffn1_w_down[l].astype(BF16),
            'ffn2_w_gu': ffn2_w_gu[l].astype(BF16), 'ffn2_w_down': ffn2_w_down[l].astype(BF16),
            'norm_mix': norm_mix[l], 'norm_ffn2': norm_ffn2[l],
            'norm_next': norm_ffn1[l + 1] if l + 1 < depth else norm_final,
            'w_in_hg': wl[:, bounds[0]:bounds[1]], 'w_in_rw': wl[:, bounds[1]:bounds[2]],
            'w_in_att': wl[:, bounds[2]:bounds[3]], 'w_in_gate': wl[:, bounds[3]:bounds[4]],
            'hg_params': jnp.stack([jnp.log(lb), jnp.log1p(-lb), 1.0 - lb, hg_gnorm[l]]),
            'rw_mu': rw_mu[l], 'rw_w0': rw_w0[l], 'rw_w_up': rw_w_up[l], 'rw_a0': rw_a0[l], 'rw_a_up': rw_a_up[l],
            'rw_g_up': rw_g_up[l], 'rw_k_k': rw_k_k[l], 'rw_k_a': rw_k_a[l], 'rw_r_k': rw_r_k[l],
            'rw_lnx_w': rw_lnx_w[l], 'rw_lnx_b': rw_lnx_b[l],
            'w_branch_hg': w_branch_hg[l].astype(BF16), 'w_branch_rw': w_branch_rw[l].astype(BF16),
            'w_branch_att': w_branch_att[l].astype(BF16), 'w_out': w_out[l].astype(BF16),
        })
    caches = (cache_att1_kv, cache_att2_kv, cache_att3_kv)
    sample_state = (state_hgrn, state_rwkv, state_rwkv_shift, caches)
    results = []
    for x3, batch, seq, state in ((x_prompt, bp, sp, None), (x_sample, bs, ss, sample_state)):
        x = x3.reshape(batch * seq, d)
        h = rmsnorm_rows(x, norm_ffn1[0], BF16)
        states = []
        for l in range(depth):
            x, h, st = _trunk_layer(x, h, layers[l], l, batch, seq, state, l + 1 == depth)
            states.append(st)
        stacked = [jnp.stack(z, axis=0) for z in zip(*states)]
        results.append((h.reshape(batch, seq, d), stacked))
    (yp, pst), (ys, sst) = results
    return (yp, ys, *pst, *sst)
```

```python
import functools
import math

import numpy as np
import jax
import jax.numpy as jnp
from jax import lax
from jax.experimental import pallas as pl
from jax.experimental.pallas import tpu as pltpu

F32 = jnp.float32
BF16 = jnp.bfloat16

HG_HEADS = 8
HG_DK = 128
RW_HEAD = 64
RW_LORA_W = 64
RW_LORA_A = 64
RW_LORA_G = 128
RW_LNX_EPS = 64e-5
ATT_GROUPS = ((128, 1), (512, 4), (2048, 16))
ATT_HEADS = 8
ATT_HD = 64
ATT_WIDTH = ATT_HEADS * ATT_HD
RMS_EPS = 1e-6
LANES = 128
SUBLANES = 8
MXU_DIM = 256
CHUNK = 64
VMEM_LIMIT = 48 * 1024 * 1024

NN = (((1,), (0,)), ((), ()))
NT = (((1,), (1,)), ((), ()))
TN = (((0,), (0,)), ((), ()))


def _dg(a, b, dims=NN):
    return lax.dot_general(a, b, dims, preferred_element_type=F32)


def _bdot(a, b, dims=NN):
    return lax.dot_general(a.astype(BF16), b.astype(BF16), dims, preferred_element_type=F32)


def _split2(x):
    hi = x.astype(BF16)
    lo = (x - hi.astype(F32)).astype(BF16)
    return hi, lo


def _split3(x):
    hi = x.astype(BF16)
    r = x - hi.astype(F32)
    mid = r.astype(BF16)
    lo = (r - mid.astype(F32)).astype(BF16)
    return hi, mid, lo


def _dot_sel(w, x, dims=NN):
    hi, mid, lo = _split3(x)
    return _dg(w, hi, dims) + _dg(w, mid, dims) + _dg(w, lo, dims)


def _dot_sel_rhs(x, w, dims=NN):
    hi, mid, lo = _split3(x)
    return _dg(hi, w, dims) + _dg(mid, w, dims) + _dg(lo, w, dims)


def _rms(x, g):
    return x * lax.rsqrt(jnp.mean(x * x, axis=-1, keepdims=True) + RMS_EPS) * g


def _sigmoid(x):
    return 1.0 / (1.0 + jnp.exp(-x))


def _params(*sem):
    return pltpu.CompilerParams(dimension_semantics=sem, vmem_limit_bytes=VMEM_LIMIT)


def _row_tile(rows, want):
    t = min(rows, want)
    assert rows % t == 0
    return t


def _rmsnorm_body(x_ref, g_ref, o_ref):
    o_ref[...] = _rms(x_ref[...], g_ref[...]).astype(o_ref.dtype)


def rmsnorm_rows(x, g, out_dtype):
    rows, d = x.shape
    tm = _row_tile(rows, 1024)
    return pl.pallas_call(
        _rmsnorm_body,
        grid=(rows // tm,),
        in_specs=[pl.BlockSpec((tm, d), lambda i: (i, 0)), pl.BlockSpec((1, d), lambda i: (0, 0))],
        out_specs=pl.BlockSpec((tm, d), lambda i: (i, 0)),
        out_shape=jax.ShapeDtypeStruct((rows, d), out_dtype),
        compiler_params=_params("parallel"),
        name="rmsnorm",
    )(x, g.reshape(1, d))


def _ffn_body(x_ref, h_ref, wg_ref, wu_ref, wd_ref, gn_ref, xo_ref, ho_ref, acc_ref):
    j = pl.program_id(1)

    @pl.when(j == 0)
    def _():
        acc_ref[...] = jnp.zeros_like(acc_ref)

    h = h_ref[...]
    gate = jnp.dot(h, wg_ref[...], preferred_element_type=F32)
    up = jnp.dot(h, wu_ref[...], preferred_element_type=F32)
    act = (gate * _sigmoid(gate) * up).astype(BF16)
    acc_ref[...] += jnp.dot(act, wd_ref[...], preferred_element_type=F32)

    @pl.when(j == pl.num_programs(1) - 1)
    def _():
        xn = x_ref[...] + 0.5 * acc_ref[...]
        xo_ref[...] = xn
        ho_ref[...] = _rms(xn, gn_ref[...]).astype(ho_ref.dtype)


def ffn_half_step(x, h, w_gu, w_down, g_next, next_dtype):
    rows, d = x.shape
    dff = w_down.shape[0]
    tm = _row_tile(rows, 1024)
    tf = 256
    nf = dff // tf
    assert dff % tf == 0
    return pl.pallas_call(
        _ffn_body,
        grid=(rows // tm, nf),
        in_specs=[
            pl.BlockSpec((tm, d), lambda i, j: (i, 0)),
            pl.BlockSpec((tm, d), lambda i, j: (i, 0)),
            pl.BlockSpec((d, tf), lambda i, j: (0, j)),
            pl.BlockSpec((d, tf), lambda i, j: (0, j + nf)),
            pl.BlockSpec((tf, d), lambda i, j: (j, 0)),
            pl.BlockSpec((1, d), lambda i, j: (0, 0)),
        ],
        out_specs=[pl.BlockSpec((tm, d), lambda i, j: (i, 0)), pl.BlockSpec((tm, d), lambda i, j: (i, 0))],
        out_shape=[jax.ShapeDtypeStruct((rows, d), F32), jax.ShapeDtypeStruct((rows, d), next_dtype)],
        scratch_shapes=[pltpu.VMEM((tm, d), F32)],
        compiler_params=_params("parallel", "arbitrary"),
        name="ffn_half_step",
    )(x, h, w_gu, w_gu, w_down, g_next.reshape(1, d))


def _proj_body(h_ref, w_ref, o_ref):
    o_ref[...] = jnp.dot(h_ref[...], w_ref[...], preferred_element_type=F32).astype(o_ref.dtype)


def project(h, w, tn, out_dtype=F32):
    rows, d = h.shape
    n = w.shape[1]
    tm = _row_tile(rows, 1024)
    assert n % tn == 0
    return pl.pallas_call(
        _proj_body,
        grid=(rows // tm, n // tn),
        in_specs=[pl.BlockSpec((tm, d), lambda i, j: (i, 0)), pl.BlockSpec((d, tn), lambda i, j: (0, j))],
        out_specs=pl.BlockSpec((tm, tn), lambda i, j: (i, j)),
        out_shape=jax.ShapeDtypeStruct((rows, n), out_dtype),
        compiler_params=_params("parallel", "parallel"),
        name="project",
    )(h, w)


def _level_constants(rows, seq):
    levels = int(math.log2(seq))
    assert 2 ** levels == seq and rows % seq == 0
    t = np.arange(rows)
    r = t[None, :]
    ws, ms = [], [np.eye(rows, dtype=bool)]
    for l in range(1, levels + 1):
        m = 2 ** l
        hm = m // 2
        par, pos = t // m, t % m
        sec = pos >= hm
        ref = par * m + hm - 1
        w = np.where(sec[:, None], (r > ref[:, None]) & (r <= t[:, None]), (r > t[:, None]) & (r <= ref[:, None]))
        ws.append(w)
        ms.append((par[:, None] == par[None, :]) & sec[:, None] & (~sec)[None, :])
    same = (t // seq)[:, None] == (t // seq)[None, :]
    ws.append(same & (r <= t[:, None]))
    ws.append(same & (r > t[:, None]))
    return np.concatenate(ws, 0).astype(np.float32), np.stack(ms).astype(np.float32)


def _hgrn_gates(f, loglb, log1mlb, onemlb):
    ls = jnp.minimum(f, 0.0) - jnp.log1p(jnp.exp(-jnp.abs(f)))
    a = log1mlb + ls
    log_f = jnp.maximum(a, loglb) + jnp.log1p(jnp.exp(-jnp.abs(a - loglb)))
    key = onemlb * _sigmoid(-f)
    return -log_f, key


def _hgrn_intra(q, k, e_lv, masks, nlev):
    a = _dg(q, k, NT) * masks[0]
    for l in range(1, nlev + 1):
        e = e_lv[l - 1]
        a = a + _dg(q * e, k * e, NT) * masks[l]
    return a


def _hgrn_prompt_body(q_ref, f_ref, i_ref, g_ref, lb_ref, w_ref, m_ref, o_ref, s_ref, st_ref, *, nlev):
    c = pl.program_id(1)
    rows = q_ref.shape[0]

    @pl.when(c == 0)
    def _():
        st_ref[...] = jnp.zeros_like(st_ref)

    nlf, key = _hgrn_gates(f_ref[...], lb_ref[0:1, :], lb_ref[1:2, :], lb_ref[2:3, :])
    dec = jnp.exp(-_dot_sel(w_ref[...], nlf))
    qraw = q_ref[...]
    qact = qraw * _sigmoid(qraw)
    val = i_ref[...]
    og = g_ref[...]
    masks = [m_ref[l] for l in range(nlev + 1)]
    for h in range(HG_HEADS):
        sl = slice(h * HG_DK, (h + 1) * HG_DK)
        q, k, v = qact[:, sl], key[:, sl], val[:, sl]
        e_lv = [dec[l * rows:(l + 1) * rows, sl] for l in range(nlev)]
        e_cum = dec[nlev * rows:(nlev + 1) * rows, sl]
        e_suf = dec[(nlev + 1) * rows:(nlev + 2) * rows, sl]
        a = _hgrn_intra(q, k, e_lv, masks, nlev)
        st = st_ref[h]
        o = _dg(q * e_cum, st, NT) + _dg(a, v, NN)
        st_new = st * e_cum[rows - 1:rows, :] + _dg(v, k * e_suf, TN)
        st_ref[h] = st_new
        y = _rms(o, lb_ref[3:4, sl])
        ogh = og[:, sl]
        o_ref[:, sl] = y * (ogh * _sigmoid(ogh))

    @pl.when(c == pl.num_programs(1) - 1)
    def _():
        for h in range(HG_HEADS):
            s_ref[0, h] = st_ref[h].T


def hgrn_prompt(c_hg, lbp, batch, seq):
    width = c_hg.shape[1] // 4
    rows = CHUNK
    nlev = int(math.log2(rows))
    wmat, masks = _level_constants(rows, rows)
    nc = seq // rows
    col = lambda j: pl.BlockSpec((rows, width), lambda b, c: (b * nc + c, j))
    return pl.pallas_call(
        functools.partial(_hgrn_prompt_body, nlev=nlev),
        grid=(batch, nc),
        in_specs=[col(0), col(1), col(2), col(3),
                  pl.BlockSpec(lbp.shape, lambda b, c: (0, 0)),
                  pl.BlockSpec(wmat.shape, lambda b, c: (0, 0)),
                  pl.BlockSpec(masks.shape, lambda b, c: (0, 0, 0))],
        out_specs=[pl.BlockSpec((rows, width), lambda b, c: (b * nc + c, 0)),
                   pl.BlockSpec((1, HG_HEADS, HG_DK, HG_DK), lambda b, c: (b, 0, 0, 0))],
        out_shape=[jax.ShapeDtypeStruct((batch * seq, width), F32),
                   jax.ShapeDtypeStruct((batch, HG_HEADS, HG_DK, HG_DK), F32)],
        scratch_shapes=[pltpu.VMEM((HG_HEADS, HG_DK, HG_DK), F32)],
        compiler_params=_params("parallel", "arbitrary"),
        name="hgrn_prompt",
    )(c_hg, c_hg, c_hg, c_hg, lbp, jnp.asarray(wmat, BF16), jnp.asarray(masks))


def _hgrn_sample_body(q_ref, f_ref, i_ref, g_ref, lb_ref, w_ref, m_ref, s0_ref, o_ref, s_ref, *, nlev, seq):
    rows = q_ref.shape[0]
    nseq = rows // seq
    nlf, key = _hgrn_gates(f_ref[...], lb_ref[0:1, :], lb_ref[1:2, :], lb_ref[2:3, :])
    dec = jnp.exp(-_dot_sel(w_ref[...], nlf))
    qraw = q_ref[...]
    q = qraw * _sigmoid(qraw)
    v = i_ref[...]
    og = g_ref[...]
    masks = [m_ref[l] for l in range(nlev + 1)]
    e_lv = [dec[l * rows:(l + 1) * rows] for l in range(nlev)]
    e_cum = dec[nlev * rows:(nlev + 1) * rows]
    e_suf = dec[(nlev + 1) * rows:(nlev + 2) * rows]
    a = _hgrn_intra(q, key, e_lv, masks, nlev)
    o_intra = _dg(a, v, NN)
    qc = q * e_cum
    ks = key * e_suf
    ones = jnp.ones((seq, HG_DK), BF16)
    outs = []
    for b in range(nseq):
        rs = slice(b * seq, (b + 1) * seq)
        s0 = s0_ref[b, 0]
        outs.append(_dg(qc[rs], s0, NN))
        total = _dot_sel_rhs(nlf[rs], ones, TN)
        s_ref[b, 0] = jnp.exp(-total) * s0 + _dg(ks[rs], v[rs], TN)
    o = o_intra + jnp.concatenate(outs, axis=0)
    y = _rms(o, lb_ref[3:4, :])
    o_ref[...] = y * (og * _sigmoid(og))


def hgrn_sample(c_hg, lbp, state, layer, batch, seq):
    width = c_hg.shape[1] // 4
    nh = width // HG_DK
    rows = CHUNK
    nseq = rows // seq
    nlev = int(math.log2(seq))
    wmat, masks = _level_constants(rows, seq)
    col = lambda j: pl.BlockSpec((rows, HG_DK), lambda i, h: (i, j * nh + h))
    lbspec = pl.BlockSpec((lbp.shape[0], HG_DK), lambda i, h: (0, h))
    return pl.pallas_call(
        functools.partial(_hgrn_sample_body, nlev=nlev, seq=seq),
        grid=(batch // nseq, nh),
        in_specs=[col(0), col(1), col(2), col(3), lbspec,
                  pl.BlockSpec(wmat.shape, lambda i, h: (0, 0)),
                  pl.BlockSpec(masks.shape, lambda i, h: (0, 0, 0)),
                  pl.BlockSpec((None, nseq, 1, HG_DK, HG_DK), lambda i, h: (layer, i, h, 0, 0))],
        out_specs=[pl.BlockSpec((rows, HG_DK), lambda i, h: (i, h)),
                   pl.BlockSpec((nseq, 1, HG_DK, HG_DK), lambda i, h: (i, h, 0, 0))],
        out_shape=[jax.ShapeDtypeStruct((batch * seq, width), F32),
                   jax.ShapeDtypeStruct((batch, nh, HG_DK, HG_DK), F32)],
        compiler_params=_params("parallel", "parallel"),
        name="hgrn_sample",
    )(c_hg, c_hg, c_hg, c_hg, lbp, jnp.asarray(wmat, BF16), jnp.asarray(masks), state)


def _group_sum(x, bd):
    hi, lo = _split2(x)
    return _dg(hi, bd, NN) + _dg(lo, bd, NN)


def _group_sum_wide(x, bd):
    return jnp.concatenate([_group_sum(x[:, j * LANES:(j + 1) * LANES], bd) for j in range(x.shape[1] // LANES)],
                           axis=1)


def _head_blockdiag(n, head):
    i = np.arange(n) // head
    return (i[:, None] == i[None, :]).astype(np.float32)


def _rwkv_consts(rows, seq):
    t = np.arange(rows)
    sq = t // seq
    same = sq[:, None] == sq[None, :]
    r = t[None, :]
    pre = same & (r <= t[:, None])
    mid = sq * seq + seq // 2 - 1
    premid = same & (r <= mid[:, None])
    suf = same & (r > t[:, None])
    wmat = np.concatenate([pre.astype(np.float32) - premid.astype(np.float32), pre, suf], 0)
    t2 = np.concatenate([t, t])
    h2 = np.concatenate([0 * t, 0 * t + 1])
    s2 = t2 // seq
    blk = (h2[:, None] == h2[None, :]) & (s2[:, None] == s2[None, :])
    strict = blk & (t2[None, :] < t2[:, None])
    incl = blk & (t2[None, :] <= t2[:, None])
    lv = []
    for l in range(1, int(math.log2(seq)) + 1):
        m = 2 ** l
        par, pos = t2 // m, t2 % m
        lv.append(blk & (par[:, None] == par[None, :]) & (pos[:, None] >= m // 2) & (pos[None, :] < m // 2))
    masks = np.stack([strict, incl, np.eye(2 * rows, dtype=bool)] + lv).astype(np.float32)
    first = (t[:, None] == (np.arange(rows // seq) * seq)[None, :]).astype(np.float32)
    return wmat, masks, first


def _stack2(x, m0, m1):
    return jnp.concatenate([x * m0, x * m1], axis=0)


def _rwkv_chunk_body(c_ref, cp_ref, sh_ref, mu_ref, vec_ref, wup_ref, aup_ref, gup_ref, bd_ref, w_ref, m_ref,
                     f_ref, r2_ref, y2_ref, bn_ref, g_ref, mx_ref, z_ref, *, seq, nchunk):
    rows = c_ref.shape[0]
    w = r2_ref.shape[1]
    npair = w // LANES
    nseq = rows // seq
    nlev = int(math.log2(seq))

    c = c_ref[...]
    rolled = pltpu.roll(c, 1, 0)
    rowi = lax.broadcasted_iota(jnp.int32, (rows, 1), 0)
    if nseq == 1:
        head_row = jnp.where(pl.program_id(0) % nchunk == 0, sh_ref[...], cp_ref[SUBLANES - 1:SUBLANES, :])
        prev = jnp.where(rowi == 0, head_row, rolled)
    else:
        prev = jnp.where(rowi % seq == 0, _dot_sel(f_ref[...], sh_ref[...]), rolled)
    xs = c + mu_ref[...] * (prev - c)

    rr, k, vv = xs[:, 0:w], xs[:, w:2 * w], xs[:, 2 * w:3 * w]
    lora = xs[:, 3 * w:3 * w + RW_LORA_W + RW_LORA_A]
    gl = xs[:, 3 * w + RW_LORA_W + RW_LORA_A:]
    w0, a0, k_k, k_a, r_k = (vec_ref[i:i + 1, :] for i in range(5))
    lw = -math.exp(-0.5) * _sigmoid(w0 + _dg(jnp.tanh(lora), wup_ref[...]))
    a = _sigmoid(a0 + _dg(lora, aup_ref[...]))
    g_ref[...] = _dg(_sigmoid(gl), gup_ref[...])
    bd = bd_ref[...]
    kkr = k * k_k
    kk = kkr / jnp.maximum(jnp.sqrt(_group_sum_wide(kkr * kkr, bd)), 1e-12)
    kx = k * (1.0 + (a - 1.0) * k_a)
    bn_ref[...] = _group_sum_wide(rr * kx * r_k, bd) * vv
    bb = kk * a
    aa = -kk

    d = _dot_sel(w_ref[...], lw)
    dm, gin, suf = d[0:rows], d[rows:2 * rows], d[2 * rows:3 * rows]
    e_inv = jnp.exp(-dm)
    e_abs = jnp.exp(gin)
    e_suf = jnp.exp(suf)
    a_mid = aa * jnp.exp(dm - lw)
    r_mid = rr * jnp.exp(dm)
    b_inv = bb * e_inv
    k_inv = kx * e_inv
    a_abs = aa * jnp.exp(gin - lw)
    r_abs = rr * e_abs
    b_end = bb * e_suf
    k_end = kx * e_suf

    lane = lax.broadcasted_iota(jnp.int32, (rows, LANES), 1)
    m0 = (lane < RW_HEAD).astype(F32)
    m1 = 1.0 - m0
    masks = [m_ref[i] for i in range(3 + nlev)]
    n2 = 2 * rows
    pairs = range(npair)
    s2 = lambda x, p: _stack2(x[:, p * LANES:(p + 1) * LANES], m0, m1)

    big = [_bdot(jnp.concatenate([s2(a_mid, p), s2(r_mid, p)], axis=0),
                 jnp.concatenate([s2(b_inv, p), s2(k_inv, p)], axis=0), NT) for p in pairs]
    nmat = [big[p][0:n2, 0:n2] * masks[0] for p in pairs]
    tinv = [masks[2] + nmat[p] * masks[3] for p in pairs]
    for l in range(2, nlev + 1):
        tinv = [tinv[p] + _bdot(_bdot(tinv[p], nmat[p] * masks[2 + l]), tinv[p]) for p in pairs]
    v_st = [s2(vv, p) for p in pairs]
    akv = [_bdot(big[p][0:n2, n2:] * masks[0], v_st[p]) for p in pairs]
    pq = [_bdot(tinv[p], jnp.concatenate([s2(a_abs, p), akv[p]], axis=1)) for p in pairs]
    ry = [_bdot(big[p][n2:, 0:n2] * masks[1], pq[p]) for p in pairs]
    rkv = [_bdot(big[p][n2:, n2:] * masks[1], v_st[p]) for p in pairs]
    eye2 = masks[2][0:RW_HEAD, 0:LANES] + masks[2][RW_HEAD:LANES, 0:LANES]
    for p in pairs:
        sl = slice(p * LANES, (p + 1) * LANES)
        r2 = s2(r_abs, p) + ry[p][:, 0:LANES]
        y2 = ry[p][:, LANES:] + rkv[p]
        r2_ref[:, sl] = r2[0:rows] + r2[rows:]
        y2_ref[:, sl] = y2[0:rows] + y2[rows:]
        b_st = s2(b_end, p)
        k_st = s2(k_end, p)
        for b in range(nseq):
            pick = lambda x: jnp.concatenate([x[b * seq:(b + 1) * seq], x[rows + b * seq:rows + (b + 1) * seq]], axis=0)
            mz = _bdot(pick(pq[p]), pick(b_st), TN)
            zz = mz[LANES:] + _bdot(pick(v_st[p]), pick(k_st), TN)
            g_end = e_abs[b * seq + seq - 1:b * seq + seq, sl]
            hs = slice(b * RW_HEAD, (b + 1) * RW_HEAD)
            mx_ref[hs, sl] = mz[0:RW_HEAD] + mz[RW_HEAD:LANES] + eye2 * g_end
            z_ref[hs, sl] = zz[0:RW_HEAD] + zz[RW_HEAD:]


def rwkv_chunk(c_rw, shift0, lp, batch, seq):
    rows_all, ncol = c_rw.shape
    w = lp['rw_w0'].shape[0]
    rows = CHUNK
    cseq = min(seq, rows)
    nseq = rows // cseq
    nchunk = seq // cseq
    wmat, masks, first = _rwkv_consts(rows, cseq)
    vecs = jnp.stack([lp['rw_w0'], lp['rw_a0'], lp['rw_k_k'], lp['rw_k_a'], lp['rw_r_k'].reshape(w)])
    wup = jnp.concatenate([lp['rw_w_up'], jnp.zeros_like(lp['rw_a_up'])], axis=0)
    aup = jnp.concatenate([jnp.zeros_like(lp['rw_w_up']), lp['rw_a_up']], axis=0)
    bd = jnp.asarray(_head_blockdiag(LANES, RW_HEAD), BF16)
    mu2 = lp['rw_mu'].reshape(1, ncol)
    if nseq == 1:
        sh = shift0.reshape(batch, 1, ncol)
        sh_spec = pl.BlockSpec((None, 1, ncol), lambda i: (i // nchunk, 0, 0))
    else:
        sh = shift0
        sh_spec = pl.BlockSpec((nseq, ncol), lambda i: (i, 0))
    consts = [mu2, vecs, wup, aup, lp['rw_g_up'], bd, jnp.asarray(wmat, BF16), jnp.asarray(masks),
              jnp.asarray(first, BF16)]
    full = lambda a: pl.BlockSpec(a.shape, lambda i: (0,) * a.ndim)
    row = pl.BlockSpec((rows, w), lambda i: (i, 0))
    st = pl.BlockSpec((nseq * RW_HEAD, w), lambda i: (i, 0))
    nst = rows_all // cseq * RW_HEAD
    per8 = rows // SUBLANES
    return pl.pallas_call(
        functools.partial(_rwkv_chunk_body, seq=cseq, nchunk=nchunk),
        grid=(rows_all // rows,),
        in_specs=[pl.BlockSpec((rows, ncol), lambda i: (i, 0)),
                  pl.BlockSpec((SUBLANES, ncol), lambda i: (jnp.maximum(i * per8 - 1, 0), 0)),
                  sh_spec] + [full(a) for a in consts],
        out_specs=[row, row, row, row, st, st],
        out_shape=[jax.ShapeDtypeStruct((rows_all, w), F32)] * 4 + [jax.ShapeDtypeStruct((nst, w), F32)] * 2,
        compiler_params=_params("parallel"),
        name="rwkv_chunk",
    )(c_rw, c_rw, sh, *consts)


def _rwkv_scan_body(r2_ref, y2_ref, mx_ref, z_ref, s0_ref, bn_ref, g_ref, vec_ref, bd_ref,
                    o_ref, s_ref, st_ref, *, seq):
    c = pl.program_id(1)
    rows = r2_ref.shape[0]
    nseq = rows // seq
    w = r2_ref.shape[1]

    @pl.when(c == 0)
    def _():
        st_ref[...] = s0_ref[...]

    lane = lax.broadcasted_iota(jnp.int32, (RW_HEAD, LANES), 1)
    m0 = (lane < RW_HEAD).astype(F32)
    m1 = 1.0 - m0
    r2, y2 = r2_ref[...], y2_ref[...]
    ys = []
    for b in range(nseq):
        rs = slice(b * seq, (b + 1) * seq)
        hs = slice(b * RW_HEAD, (b + 1) * RW_HEAD)
        yb = []
        for p in range(w // LANES):
            sl = slice(p * LANES, (p + 1) * LANES)
            s = st_ref[b, :, sl]
            yb.append(_dg(r2[rs, sl], _stack2(s, m0, m1), NT) + y2[rs, sl])
            st_ref[b, :, sl] = _dg(s, _stack2(mx_ref[hs, sl], m0, m1), NN) + z_ref[hs, sl]
        ys.append(jnp.concatenate(yb, axis=1))
    y = jnp.concatenate(ys, axis=0) if nseq > 1 else ys[0]
    bd = bd_ref[...]
    mean = _group_sum_wide(y, bd) * (1.0 / RW_HEAD)
    yc = y - mean
    var = _group_sum_wide(yc * yc, bd) * (1.0 / RW_HEAD)
    yn = yc * lax.rsqrt(var + RW_LNX_EPS) * vec_ref[0:1, :] + vec_ref[1:2, :]
    o_ref[...] = (yn + bn_ref[...]) * g_ref[...]

    @pl.when(c == pl.num_programs(1) - 1)
    def _():
        s_ref[...] = st_ref[...]


def rwkv_scan(r2, y2, mx, z, s0, bonus, g, vecs, batch, seq):
    rows_all, w = r2.shape
    rows = CHUNK
    cseq = min(seq, rows)
    nseq = rows // cseq
    nc = seq // cseq
    bd = jnp.asarray(_head_blockdiag(LANES, RW_HEAD), BF16)
    row = pl.BlockSpec((rows, w), lambda i, c: (i * nc + c, 0))
    st = pl.BlockSpec((nseq * RW_HEAD, w), lambda i, c: (i * nc + c, 0))
    sblk = pl.BlockSpec((nseq, RW_HEAD, w), lambda i, c: (i, 0, 0))
    return pl.pallas_call(
        functools.partial(_rwkv_scan_body, seq=cseq),
        grid=(batch // nseq, nc),
        in_specs=[row, row, st, st, sblk, row, row,
                  pl.BlockSpec(vecs.shape, lambda i, c: (0, 0)), pl.BlockSpec(bd.shape, lambda i, c: (0, 0))],
        out_specs=[row, sblk],
        out_shape=[jax.ShapeDtypeStruct((rows_all, w), F32), jax.ShapeDtypeStruct(s0.shape, F32)],
        scratch_shapes=[pltpu.VMEM((nseq, RW_HEAD, w), F32)],
        compiler_params=_params("parallel", "arbitrary"),
        name="rwkv_scan",
    )(r2, y2, mx, z, s0, bonus, g, vecs, bd)


def rwkv_branch(c_rw, shift0, s0, lp, batch, seq):
    ncol = c_rw.shape[1]
    w = lp['rw_w0'].shape[0]
    nh = w // RW_HEAD
    r2, y2, bonus, g, mx, z = rwkv_chunk(c_rw, shift0, lp, batch, seq)
    s0t = jnp.transpose(s0, (0, 2, 1, 3)).reshape(batch, RW_HEAD, w)
    ovecs = jnp.stack([lp['rw_lnx_w'], lp['rw_lnx_b']])
    out, s_fin = rwkv_scan(r2, y2, mx, z, s0t, bonus, g, ovecs, batch, seq)
    s_fin = jnp.transpose(s_fin.reshape(batch, RW_HEAD, nh, RW_HEAD), (0, 2, 1, 3))
    return out, s_fin, c_rw.reshape(batch, seq, ncol)[:, -1]


def _att_prompt_body(q_ref, kp_ref, kc_ref, vp_ref, vc_ref, o_ref, l_ref, *, sub, dil):
    n = pl.program_id(1)
    scale = ATT_HD ** -0.5
    qi = lax.broadcasted_iota(jnp.int32, (sub, 2 * sub), 0)
    ki = lax.broadcasted_iota(jnp.int32, (sub, 2 * sub), 1) - sub
    dist = qi - ki
    valid = (dist >= 0) & (dist <= sub) & (n * sub + ki >= 0)
    lane = lax.broadcasted_iota(jnp.int32, (1, LANES), 1)
    hm = [(lane < ATT_HD).astype(F32), (lane >= ATT_HD).astype(F32)]
    for r in range(dil):
        rs = pl.ds(r, sub, stride=dil) if dil > 1 else slice(None)
        for p in range(q_ref.shape[1] // LANES):
            sl = slice(p * LANES, (p + 1) * LANES)
            q = q_ref[rs, sl] * scale
            kcat = jnp.concatenate([kp_ref[rs, sl], kc_ref[rs, sl]], axis=0).astype(BF16)
            vcat = jnp.concatenate([vp_ref[rs, sl], vc_ref[rs, sl]], axis=0)
            o = jnp.zeros((sub, LANES), F32)
            lse = jnp.zeros((sub, LANES), F32)
            for m in hm:
                s = _dg((q * m).astype(BF16), kcat, NT)
                s = jnp.where(valid, s, -jnp.inf)
                mx = jnp.max(s, axis=-1, keepdims=True)
                pr = jnp.exp(s - mx)
                den = jnp.sum(pr, axis=-1, keepdims=True)
                o = o + _bdot(pr, vcat * m, NN) / den
                lse = lse + (mx + jnp.log(den)) * m
            o_ref[rs, sl] = o
            l_ref[rs, sl] = lse


def att_prompt_group(c_att, gi, batch, seq):
    win, dil = ATT_GROUPS[gi]
    sub = win // dil
    nblk = seq // win
    assert seq % win == 0
    cw = ATT_WIDTH if dil == 1 else LANES
    ppw = ATT_WIDTH // cw
    cur = lambda j: pl.BlockSpec((win, cw), lambda b, n, p: (b * nblk + n, (3 * gi + j) * ppw + p))
    prv = lambda j: pl.BlockSpec((win, cw),
                                 lambda b, n, p: (b * nblk + jnp.maximum(n - 1, 0), (3 * gi + j) * ppw + p))
    outb = pl.BlockSpec((win, cw), lambda b, n, p: (b * nblk + n, p))
    return pl.pallas_call(
        functools.partial(_att_prompt_body, sub=sub, dil=dil),
        grid=(batch, nblk, ppw),
        in_specs=[cur(0), prv(1), cur(1), prv(2), cur(2)],
        out_specs=[outb, outb],
        out_shape=[jax.ShapeDtypeStruct((batch * seq, ATT_WIDTH), F32)] * 2,
        compiler_params=_params("parallel", "parallel", "parallel"),
        name="att_prompt",
    )(c_att, c_att, c_att, c_att, c_att)


def _att_sample_body(q_ref, kn_ref, vn_ref, c_ref, bias_ref, o_ref, l_ref, *, dil):
    seq = q_ref.shape[0]
    hpq = MXU_DIM // ATT_HD
    q = q_ref[...] * (ATT_HD ** -0.5)
    kn, vn = kn_ref[...], vn_ref[...]
    trow = lax.broadcasted_iota(jnp.int32, (hpq * seq, seq), 0) % seq
    tcol = lax.broadcasted_iota(jnp.int32, (hpq * seq, seq), 1)
    new_ok = (tcol <= trow) & ((trow - tcol) % dil == 0)
    lane_head = lax.broadcasted_iota(jnp.int32, (seq, MXU_DIM), 1) // ATT_HD
    hm = [(lane_head == h).astype(F32) for h in range(hpq)]
    bias = bias_ref[...]
    outs, lses = [], []
    for g in range(ATT_WIDTH // MXU_DIM):
        cs = slice(g * MXU_DIM, (g + 1) * MXU_DIM)
        qst = jnp.concatenate([q[:, cs] * m for m in hm], axis=0)
        sc = _bdot(qst, c_ref[0, 0, cs, :], NN) + bias
        sn = jnp.where(new_ok, _bdot(qst, kn[:, cs], NT), -jnp.inf)
        mx = jnp.maximum(jnp.max(sc, axis=-1, keepdims=True), jnp.max(sn, axis=-1, keepdims=True))
        pc = jnp.exp(sc - mx)
        pn = jnp.exp(sn - mx)
        den = jnp.sum(pc, axis=-1, keepdims=True) + jnp.sum(pn, axis=-1, keepdims=True)
        o = (_bdot(pc, c_ref[0, 1, cs, :], NT) + _bdot(pn, vn[:, cs], NN)) / den
        lse = mx + jnp.log(den)
        outs.append(sum(o[h * seq:(h + 1) * seq] * hm[h] for h in range(hpq)))
        lses.append(sum(lse[h * seq:(h + 1) * seq] * hm[h] for h in range(hpq)))
    o_ref[...] = jnp.concatenate(outs, axis=1)
    l_ref[...] = jnp.concatenate(lses, axis=1)


def att_sample_group(c_att, cache, layer, gi, batch, seq):
    win, dil = ATT_GROUPS[gi]
    depth, _, length = cache.shape[:3]
    assert length == win and seq <= win // dil
    cv = jnp.transpose(cache, (0, 1, 3, 4, 5, 2)).reshape(depth, batch, 2, ATT_WIDTH, length)
    hpq = MXU_DIM // ATT_HD
    t = np.tile(np.arange(seq), hpq)[:, None]
    j = np.arange(length)[None, :]
    bias = jnp.asarray(np.where((j >= t) & ((j - t) % dil == 0), 0.0, -np.inf), F32)
    new = lambda c: pl.BlockSpec((seq, ATT_WIDTH), lambda b: (b, 3 * gi + c))
    outb = pl.BlockSpec((seq, ATT_WIDTH), lambda b: (b, 0))
    return pl.pallas_call(
        functools.partial(_att_sample_body, dil=dil),
        grid=(batch,),
        in_specs=[new(0), new(1), new(2),
                  pl.BlockSpec((None, 1, 2, ATT_WIDTH, length), lambda b: (layer, b, 0, 0, 0)),
                  pl.BlockSpec(bias.shape, lambda b: (0, 0))],
        out_specs=[outb, outb],
        out_shape=[jax.ShapeDtypeStruct((batch * seq, ATT_WIDTH), F32)] * 2,
        compiler_params=_params("parallel"),
        name="att_sample",
    )(c_att, c_att, c_att, cv, bias)


def _merge_body(x_ref, gate_ref, hg_ref, rw_ref, o1_ref, o2_ref, o3_ref, l1_ref, l2_ref, l3_ref,
                whg_ref, wrw_ref, watt_ref, wout_ref, gn_ref, xo_ref, ho_ref):
    d = x_ref.shape[1]
    l1, l2, l3 = l1_ref[...], l2_ref[...], l3_ref[...]
    mx = jnp.maximum(jnp.maximum(l1, l2), l3)
    e1, e2, e3 = jnp.exp(l1 - mx), jnp.exp(l2 - mx), jnp.exp(l3 - mx)
    att = (e1 * o1_ref[...] + e2 * o2_ref[...] + e3 * o3_ref[...]) / (e1 + e2 + e3)
    gates = _sigmoid(gate_ref[...].astype(F32))
    bm = lambda a, w_ref: jnp.dot(a.astype(BF16), w_ref[...], preferred_element_type=F32)
    merged = (gates[:, 0:d] * bm(hg_ref[...], whg_ref)
              + gates[:, d:2 * d] * bm(rw_ref[...], wrw_ref)
              + gates[:, 2 * d:3 * d] * bm(att, watt_ref))
    xn = x_ref[...] + bm(merged, wout_ref)
    xo_ref[...] = xn
    ho_ref[...] = _rms(xn, gn_ref[...]).astype(ho_ref.dtype)


def merge_branches(x, c_gate, o_hg, o_rw, att_o, att_l, w_hg, w_rw, w_att, w_out, g_next):
    rows, d = x.shape
    tm = _row_tile(rows, 256)
    row = lambda a: pl.BlockSpec((tm, a.shape[1]), lambda i: (i, 0))
    full = lambda a: pl.BlockSpec(a.shape, lambda i: (0, 0))
    g2 = g_next.reshape(1, d)
    args = [x, c_gate, o_hg, o_rw, *att_o, *att_l]
    wts = [w_hg, w_rw, w_att, w_out, g2]
    return pl.pallas_call(
        _merge_body,
        grid=(rows // tm,),
        in_specs=[row(a) for a in args] + [full(a) for a in wts],
        out_specs=[pl.BlockSpec((tm, d), lambda i: (i, 0))] * 2,
        out_shape=[jax.ShapeDtypeStruct((rows, d), F32), jax.ShapeDtypeStruct((rows, d), BF16)],
        compiler_params=_params("parallel"),
        name="merge_branches",
    )(*args, *wts)


def _trunk_layer(x, h, lw, layer, batch, seq, state, last):
    d = x.shape[1]
    x, h = ffn_half_step(x, h, lw['ffn1_w_gu'], lw['ffn1_w_down'], lw['norm_mix'], BF16)
    c_hg = project(h, lw['w_in_hg'], 512)
    c_rw = project(h, lw['w_in_rw'], 256)
    c_att = project(h, lw['w_in_att'], 512)
    c_gate = project(h, lw['w_in_gate'], 512, BF16)
    if state is None:
        o_hg, s_hg = hgrn_prompt(c_hg, lw['hg_params'], batch, seq)
        shift0 = jnp.zeros((batch, c_rw.shape[1]), F32)
        s_rw0 = jnp.zeros((batch, d // RW_HEAD, RW_HEAD, RW_HEAD), F32)
    else:
        s_hgrn, s_rwkv, s_shift, caches = state
        o_hg, s_hg = hgrn_sample(c_hg, lw['hg_params'], s_hgrn, layer, batch, seq)
        shift0, s_rw0 = s_shift[layer], s_rwkv[layer]
    o_rw, s_rw, shift = rwkv_branch(c_rw, shift0, s_rw0, lw, batch, seq)
    att_o, att_l, kv_rows = [], [], []
    c_att3 = c_att.reshape(batch, seq, -1)
    for gi, (win, dil) in enumerate(ATT_GROUPS):
        if state is None:
            o, l = att_prompt_group(c_att, gi, batch, seq)
        else:
            o, l = att_sample_group(c_att, caches[gi], layer, gi, batch, seq)
        att_o.append(o)
        att_l.append(l)
        keep = min(win, seq)
        kv = c_att3[:, seq - keep:, (3 * gi + 1) * ATT_WIDTH:(3 * gi + 3) * ATT_WIDTH]
        kv_rows.append(kv.reshape(batch, keep, 2, ATT_HEADS, ATT_HD))
    x, h = merge_branches(x, c_gate, o_hg, o_rw, att_o, att_l, lw['w_branch_hg'], lw['w_branch_rw'],
                          lw['w_branch_att'], lw['w_out'], lw['norm_ffn2'])
    x, h = ffn_half_step(x, h, lw['ffn2_w_gu'], lw['ffn2_w_down'], lw['norm_next'], F32 if last else BF16)
    return x, h, (s_hg, s_rw, shift, *kv_rows)


def _lower_bounds(hg_lb):
    p = jax.nn.softmax(hg_lb.astype(F32), axis=0)
    c = jnp.cumsum(p, axis=0)
    return c - c[0:1]


def kernel(x_prompt, x_sample, state_hgrn, state_rwkv, state_rwkv_shift, cache_att1_kv, cache_att2_kv, cache_att3_kv, norm_ffn1, ffn1_w_gu, ffn1_w_down, norm_mix, w_in, hg_lb, hg_gnorm, rw_mu, rw_w0, rw_w_up, rw_a0, rw_a_up, rw_g_up, rw_k_k, rw_k_a, rw_r_k, rw_lnx_w, rw_lnx_b, w_branch_hg, w_branch_rw, w_branch_att, w_out, norm_ffn2, ffn2_w_gu, ffn2_w_down, norm_final):
    depth = norm_ffn1.shape[0]
    bp, sp, d = x_prompt.shape
    bs, ss, _ = x_sample.shape
    hg_cols = 2 * HG_HEADS * HG_DK + 2 * hg_gnorm.shape[1]
    rw_cols = rw_mu.shape[1]
    att_cols = 3 * len(ATT_GROUPS) * ATT_WIDTH
    bounds = np.cumsum([0, hg_cols, rw_cols, att_cols, 3 * d])
    assert bounds[-1] == w_in.shape[2]
    lbs = _lower_bounds(hg_lb)
    layers = []
    for l in range(depth):
        lb = lbs[l]
        wl = w_in[l].astype(BF16)
        layers.append({
            'ffn1_w_gu': ffn1_w_gu[l].astype(BF16), 'ffn1_w_down': ffn1_w_down[l].astype(BF16),
            'ffn2_w_gu': ffn2_w_gu[l].astype(BF16), 'ffn2_w_down': ffn2_w_down[l].astype(BF16),
            'norm_mix': norm_mix[l], 'norm_ffn2': norm_ffn2[l],
            'norm_next': norm_ffn1[l + 1] if l + 1 < depth else norm_final,
            'w_in_hg': wl[:, bounds[0]:bounds[1]], 'w_in_rw': wl[:, bounds[1]:bounds[2]],
            'w_in_att': wl[:, bounds[2]:bounds[3]], 'w_in_gate': wl[:, bounds[3]:bounds[4]],
            'hg_params': jnp.stack([jnp.log(lb), jnp.log1p(-lb), 1.0 - lb, hg_gnorm[l]]),
            'rw_mu': rw_mu[l], 'rw_w0': rw_w0[l], 'rw_w_up': rw_w_up[l], 'rw_a0': rw_a0[l], 'rw_a_up': rw_a_up[l],
            'rw_g_up': rw_g_up[l], 'rw_k_k': rw_k_k[l], 'rw_k_a': rw_k_a[l], 'rw_r_k': rw_r_k[l],
            'rw_lnx_w': rw_lnx_w[l], 'rw_lnx_b': rw_lnx_b[l],
            'w_branch_hg': w_branch_hg[l].astype(BF16), 'w_branch_rw': w_branch_rw[l].astype(BF16),
            'w_branch_att': w_branch_att[l].astype(BF16), 'w_out': w_out[l].astype(BF16),
        })
    caches = (cache_att1_kv, cache_att2_kv, cache_att3_kv)
    sample_state = (state_hgrn, state_rwkv, state_rwkv_shift, caches)
    results = []
    for x3, batch, seq, state in ((x_prompt, bp, sp, None), (x_sample, bs, ss, sample_state)):
        x = x3.reshape(batch * seq, d)
        h = rmsnorm_rows(x, norm_ffn1[0], BF16)
        states = []
        for l in range(depth):
            x, h, st = _trunk_layer(x, h, layers[l], l, batch, seq, state, l + 1 == depth)
            states.append(st)
        stacked = [jnp.stack(z, axis=0) for z in zip(*states)]
        results.append((h.reshape(batch, seq, d), stacked))
    (yp, pst), (ys, sst) = results
    return (yp, ys, *pst, *sst)
```

```python
import functools
import math

import numpy as np
import jax
import jax.numpy as jnp
from jax import lax
from jax.experimental import pallas as pl
from jax.experimental.pallas import tpu as pltpu

F32 = jnp.float32
BF16 = jnp.bfloat16

HG_HEADS = 8
HG_DK = 128
RW_HEAD = 64
RW_LORA_W = 64
RW_LORA_A = 64
RW_LORA_G = 128
RW_LNX_EPS = 64e-5
ATT_GROUPS = ((128, 1), (512, 4), (2048, 16))
ATT_HEADS = 8
ATT_HD = 64
ATT_WIDTH = ATT_HEADS * ATT_HD
RMS_EPS = 1e-6
LANES = 128
SUBLANES = 8
MXU_DIM = 256
CHUNK = 64
VMEM_LIMIT = 48 * 1024 * 1024

NN = (((1,), (0,)), ((), ()))
NT = (((1,), (1,)), ((), ()))
TN = (((0,), (0,)), ((), ()))


def _dg(a, b, dims=NN):
    return lax.dot_general(a, b, dims, preferred_element_type=F32)


def _bdot(a, b, dims=NN):
    return lax.dot_general(a.astype(BF16), b.astype(BF16), dims, preferred_element_type=F32)


def _split2(x):
    hi = x.astype(BF16)
    lo = (x - hi.astype(F32)).astype(BF16)
    return hi, lo


def _split3(x):
    hi = x.astype(BF16)
    r = x - hi.astype(F32)
    mid = r.astype(BF16)
    lo = (r - mid.astype(F32)).astype(BF16)
    return hi, mid, lo


def _dot_sel(w, x, dims=NN):
    hi, mid, lo = _split3(x)
    return _dg(w, hi, dims) + _dg(w, mid, dims) + _dg(w, lo, dims)


def _dot_sel_rhs(x, w, dims=NN):
    hi, mid, lo = _split3(x)
    return _dg(hi, w, dims) + _dg(mid, w, dims) + _dg(lo, w, dims)


def _rms(x, g):
    return x * lax.rsqrt(jnp.mean(x * x, axis=-1, keepdims=True) + RMS_EPS) * g


def _sigmoid(x):
    return 1.0 / (1.0 + jnp.exp(-x))


def _params(*sem):
    return pltpu.CompilerParams(dimension_semantics=sem, vmem_limit_bytes=VMEM_LIMIT)


def _row_tile(rows, want):
    t = min(rows, want)
    assert rows % t == 0
    return t


def _rmsnorm_body(x_ref, g_ref, o_ref):
    o_ref[...] = _rms(x_ref[...], g_ref[...]).astype(o_ref.dtype)


def rmsnorm_rows(x, g, out_dtype):
    rows, d = x.shape
    tm = _row_tile(rows, 1024)
    return pl.pallas_call(
        _rmsnorm_body,
        grid=(rows // tm,),
        in_specs=[pl.BlockSpec((tm, d), lambda i: (i, 0)), pl.BlockSpec((1, d), lambda i: (0, 0))],
        out_specs=pl.BlockSpec((tm, d), lambda i: (i, 0)),
        out_shape=jax.ShapeDtypeStruct((rows, d), out_dtype),
        compiler_params=_params("parallel"),
        name="rmsnorm",
    )(x, g.reshape(1, d))


def _ffn_body(x_ref, h_ref, wg_ref, wu_ref, wd_ref, gn_ref, xo_ref, ho_ref, acc_ref):
    j = pl.program_id(1)

    @pl.when(j == 0)
    def _():
        acc_ref[...] = jnp.zeros_like(acc_ref)

    h = h_ref[...]
    gate = jnp.dot(h, wg_ref[...], preferred_element_type=F32)
    up = jnp.dot(h, wu_ref[...], preferred_element_type=F32)
    act = (gate * _sigmoid(gate) * up).astype(BF16)
    acc_ref[...] += jnp.dot(act, wd_ref[...], preferred_element_type=F32)

    @pl.when(j == pl.num_programs(1) - 1)
    def _():
        xn = x_ref[...] + 0.5 * acc_ref[...]
        xo_ref[...] = xn
        ho_ref[...] = _rms(xn, gn_ref[...]).astype(ho_ref.dtype)


def ffn_half_step(x, h, w_gu, w_down, g_next, next_dtype):
    rows, d = x.shape
    dff = w_down.shape[0]
    tm = _row_tile(rows, 1024)
    tf = 256
    nf = dff // tf
    assert dff % tf == 0
    return pl.pallas_call(
        _ffn_body,
        grid=(rows // tm, nf),
        in_specs=[
            pl.BlockSpec((tm, d), lambda i, j: (i, 0)),
            pl.BlockSpec((tm, d), lambda i, j: (i, 0)),
            pl.BlockSpec((d, tf), lambda i, j: (0, j)),
            pl.BlockSpec((d, tf), lambda i, j: (0, j + nf)),
            pl.BlockSpec((tf, d), lambda i, j: (j, 0)),
            pl.BlockSpec((1, d), lambda i, j: (0, 0)),
        ],
        out_specs=[pl.BlockSpec((tm, d), lambda i, j: (i, 0)), pl.BlockSpec((tm, d), lambda i, j: (i, 0))],
        out_shape=[jax.ShapeDtypeStruct((rows, d), F32), jax.ShapeDtypeStruct((rows, d), next_dtype)],
        scratch_shapes=[pltpu.VMEM((tm, d), F32)],
        compiler_params=_params("parallel", "arbitrary"),
        name="ffn_half_step",
    )(x, h, w_gu, w_gu, w_down, g_next.reshape(1, d))


def _proj_body(h_ref, w_ref, o_ref):
    o_ref[...] = jnp.dot(h_ref[...], w_ref[...], preferred_element_type=F32).astype(o_ref.dtype)


def project(h, w, tn, out_dtype=F32):
    rows, d = h.shape
    n = w.shape[1]
    tm = _row_tile(rows, 2048)
    assert n % tn == 0
    return pl.pallas_call(
        _proj_body,
        grid=(rows // tm, n // tn),
        in_specs=[pl.BlockSpec((tm, d), lambda i, j: (i, 0)), pl.BlockSpec((d, tn), lambda i, j: (0, j))],
        out_specs=pl.BlockSpec((tm, tn), lambda i, j: (i, j)),
        out_shape=jax.ShapeDtypeStruct((rows, n), out_dtype),
        compiler_params=_params("parallel", "parallel"),
        name="project",
    )(h, w)


def _level_constants(rows, seq):
    levels = int(math.log2(seq))
    assert 2 ** levels == seq and rows % seq == 0
    t = np.arange(rows)
    r = t[None, :]
    ws, ms = [], [np.eye(rows, dtype=bool)]
    for l in range(1, levels + 1):
        m = 2 ** l
        hm = m // 2
        par, pos = t // m, t % m
        sec = pos >= hm
        ref = par * m + hm - 1
        w = np.where(sec[:, None], (r > ref[:, None]) & (r <= t[:, None]), (r > t[:, None]) & (r <= ref[:, None]))
        ws.append(w)
        ms.append((par[:, None] == par[None, :]) & sec[:, None] & (~sec)[None, :])
    same = (t // seq)[:, None] == (t // seq)[None, :]
    ws.append(same & (r <= t[:, None]))
    ws.append(same & (r > t[:, None]))
    return np.concatenate(ws, 0).astype(np.float32), np.stack(ms).astype(np.float32)


def _hgrn_gates(f, loglb, log1mlb, onemlb):
    ls = jnp.minimum(f, 0.0) - jnp.log1p(jnp.exp(-jnp.abs(f)))
    a = log1mlb + ls
    log_f = jnp.maximum(a, loglb) + jnp.log1p(jnp.exp(-jnp.abs(a - loglb)))
    key = onemlb * _sigmoid(-f)
    return -log_f, key


def _hgrn_intra(q, k, e_lv, masks, nlev):
    a = _dg(q, k, NT) * masks[0]
    for l in range(1, nlev + 1):
        e = e_lv[l - 1]
        a = a + _dg(q * e, k * e, NT) * masks[l]
    return a


def _hgrn_prompt_body(q_ref, f_ref, i_ref, g_ref, lb_ref, w_ref, m_ref, o_ref, s_ref, st_ref, *, nlev):
    c = pl.program_id(1)
    rows = q_ref.shape[0]

    @pl.when(c == 0)
    def _():
        st_ref[...] = jnp.zeros_like(st_ref)

    nlf, key = _hgrn_gates(f_ref[...], lb_ref[0:1, :], lb_ref[1:2, :], lb_ref[2:3, :])
    dec = jnp.exp(-_dot_sel(w_ref[...], nlf))
    qraw = q_ref[...]
    qact = qraw * _sigmoid(qraw)
    val = i_ref[...]
    og = g_ref[...]
    masks = [m_ref[l] for l in range(nlev + 1)]
    for h in range(HG_HEADS):
        sl = slice(h * HG_DK, (h + 1) * HG_DK)
        q, k, v = qact[:, sl], key[:, sl], val[:, sl]
        e_lv = [dec[l * rows:(l + 1) * rows, sl] for l in range(nlev)]
        e_cum = dec[nlev * rows:(nlev + 1) * rows, sl]
        e_suf = dec[(nlev + 1) * rows:(nlev + 2) * rows, sl]
        a = _hgrn_intra(q, k, e_lv, masks, nlev)
        st = st_ref[h]
        o = _dg(q * e_cum, st, NT) + _dg(a, v, NN)
        st_new = st * e_cum[rows - 1:rows, :] + _dg(v, k * e_suf, TN)
        st_ref[h] = st_new
        y = _rms(o, lb_ref[3:4, sl])
        ogh = og[:, sl]
        o_ref[:, sl] = (y * (ogh * _sigmoid(ogh))).astype(o_ref.dtype)

    @pl.when(c == pl.num_programs(1) - 1)
    def _():
        for h in range(HG_HEADS):
            s_ref[0, h] = st_ref[h].T


def hgrn_prompt(c_hg, lbp, batch, seq):
    width = c_hg.shape[1] // 4
    rows = CHUNK
    nlev = int(math.log2(rows))
    wmat, masks = _level_constants(rows, rows)
    nc = seq // rows
    col = lambda j: pl.BlockSpec((rows, width), lambda b, c: (b * nc + c, j))
    return pl.pallas_call(
        functools.partial(_hgrn_prompt_body, nlev=nlev),
        grid=(batch, nc),
        in_specs=[col(0), col(1), col(2), col(3),
                  pl.BlockSpec(lbp.shape, lambda b, c: (0, 0)),
                  pl.BlockSpec(wmat.shape, lambda b, c: (0, 0)),
                  pl.BlockSpec(masks.shape, lambda b, c: (0, 0, 0))],
        out_specs=[pl.BlockSpec((rows, width), lambda b, c: (b * nc + c, 0)),
                   pl.BlockSpec((1, HG_HEADS, HG_DK, HG_DK), lambda b, c: (b, 0, 0, 0))],
        out_shape=[jax.ShapeDtypeStruct((batch * seq, width), BF16),
                   jax.ShapeDtypeStruct((batch, HG_HEADS, HG_DK, HG_DK), F32)],
        scratch_shapes=[pltpu.VMEM((HG_HEADS, HG_DK, HG_DK), F32)],
        compiler_params=_params("parallel", "arbitrary"),
        name="hgrn_prompt",
    )(c_hg, c_hg, c_hg, c_hg, lbp, jnp.asarray(wmat, BF16), jnp.asarray(masks))


def _hgrn_sample_body(q_ref, f_ref, i_ref, g_ref, lb_ref, w_ref, m_ref, s0_ref, o_ref, s_ref, *, nlev, seq):
    rows = q_ref.shape[0]
    nseq = rows // seq
    nlf, key = _hgrn_gates(f_ref[...], lb_ref[0:1, :], lb_ref[1:2, :], lb_ref[2:3, :])
    dec = jnp.exp(-_dot_sel(w_ref[...], nlf))
    qraw = q_ref[...]
    q = qraw * _sigmoid(qraw)
    v = i_ref[...]
    og = g_ref[...]
    masks = [m_ref[l] for l in range(nlev + 1)]
    e_lv = [dec[l * rows:(l + 1) * rows] for l in range(nlev)]
    e_cum = dec[nlev * rows:(nlev + 1) * rows]
    e_suf = dec[(nlev + 1) * rows:(nlev + 2) * rows]
    a = _hgrn_intra(q, key, e_lv, masks, nlev)
    o_intra = _dg(a, v, NN)
    qc = q * e_cum
    ks = key * e_suf
    ones = jnp.ones((seq, HG_DK), BF16)
    outs = []
    for b in range(nseq):
        rs = slice(b * seq, (b + 1) * seq)
        s0 = s0_ref[b, 0]
        outs.append(_dg(qc[rs], s0, NN))
        total = _dot_sel_rhs(nlf[rs], ones, TN)
        s_ref[b, 0] = jnp.exp(-total) * s0 + _dg(ks[rs], v[rs], TN)
    o = o_intra + jnp.concatenate(outs, axis=0)
    y = _rms(o, lb_ref[3:4, :])
    o_ref[...] = (y * (og * _sigmoid(og))).astype(o_ref.dtype)


def hgrn_sample(c_hg, lbp, state, layer, batch, seq):
    width = c_hg.shape[1] // 4
    nh = width // HG_DK
    rows = CHUNK
    nseq = rows // seq
    nlev = int(math.log2(seq))
    wmat, masks = _level_constants(rows, seq)
    col = lambda j: pl.BlockSpec((rows, HG_DK), lambda i, h: (i, j * nh + h))
    lbspec = pl.BlockSpec((lbp.shape[0], HG_DK), lambda i, h: (0, h))
    return pl.pallas_call(
        functools.partial(_hgrn_sample_body, nlev=nlev, seq=seq),
        grid=(batch // nseq, nh),
        in_specs=[col(0), col(1), col(2), col(3), lbspec,
                  pl.BlockSpec(wmat.shape, lambda i, h: (0, 0)),
                  pl.BlockSpec(masks.shape, lambda i, h: (0, 0, 0)),
                  pl.BlockSpec((None, nseq, 1, HG_DK, HG_DK), lambda i, h: (layer, i, h, 0, 0))],
        out_specs=[pl.BlockSpec((rows, HG_DK), lambda i, h: (i, h)),
                   pl.BlockSpec((nseq, 1, HG_DK, HG_DK), lambda i, h: (i, h, 0, 0))],
        out_shape=[jax.ShapeDtypeStruct((batch * seq, width), BF16),
                   jax.ShapeDtypeStruct((batch, nh, HG_DK, HG_DK), F32)],
        compiler_params=_params("parallel", "parallel"),
        name="hgrn_sample",
    )(c_hg, c_hg, c_hg, c_hg, lbp, jnp.asarray(wmat, BF16), jnp.asarray(masks), state)


def _group_sum(x, bd):
    hi, lo = _split2(x)
    return _dg(hi, bd, NN) + _dg(lo, bd, NN)


def _group_sum_wide(x, bd):
    return jnp.concatenate([_group_sum(x[:, j * LANES:(j + 1) * LANES], bd) for j in range(x.shape[1] // LANES)],
                           axis=1)


def _head_blockdiag(n, head):
    i = np.arange(n) // head
    return (i[:, None] == i[None, :]).astype(np.float32)


def _rwkv_consts(rows, seq):
    t = np.arange(rows)
    sq = t // seq
    same = sq[:, None] == sq[None, :]
    r = t[None, :]
    pre = same & (r <= t[:, None])
    mid = sq * seq + seq // 2 - 1
    premid = same & (r <= mid[:, None])
    suf = same & (r > t[:, None])
    wmat = np.concatenate([pre.astype(np.float32) - premid.astype(np.float32), pre, suf], 0)
    t2 = np.concatenate([t, t])
    h2 = np.concatenate([0 * t, 0 * t + 1])
    s2 = t2 // seq
    blk = (h2[:, None] == h2[None, :]) & (s2[:, None] == s2[None, :])
    strict = blk & (t2[None, :] < t2[:, None])
    incl = blk & (t2[None, :] <= t2[:, None])
    lv = []
    for l in range(1, int(math.log2(seq)) + 1):
        m = 2 ** l
        par, pos = t2 // m, t2 % m
        lv.append(blk & (par[:, None] == par[None, :]) & (pos[:, None] >= m // 2) & (pos[None, :] < m // 2))
    masks = np.stack([strict, incl, np.eye(2 * rows, dtype=bool)] + lv).astype(np.float32)
    first = (t[:, None] == (np.arange(rows // seq) * seq)[None, :]).astype(np.float32)
    return wmat, masks, first


def _stack2(x, m0, m1):
    return jnp.concatenate([x * m0, x * m1], axis=0)


def _rwkv_chunk_body(c_ref, cp_ref, sh_ref, mu_ref, vec_ref, wup_ref, aup_ref, gup_ref, bd_ref, w_ref, m_ref,
                     f_ref, r2_ref, y2_ref, bn_ref, g_ref, mx_ref, z_ref, *, seq, nchunk):
    rows = c_ref.shape[0]
    w = r2_ref.shape[1]
    npair = w // LANES
    nseq = rows // seq
    nlev = int(math.log2(seq))

    c = c_ref[...].astype(F32)
    rolled = pltpu.roll(c, 1, 0)
    rowi = lax.broadcasted_iota(jnp.int32, (rows, 1), 0)
    if nseq == 1:
        last = cp_ref.shape[0] - 1
        head_row = jnp.where(pl.program_id(0) % nchunk == 0, sh_ref[...], cp_ref[last:last + 1, :].astype(F32))
        prev = jnp.where(rowi == 0, head_row, rolled)
    else:
        prev = jnp.where(rowi % seq == 0, _dot_sel(f_ref[...], sh_ref[...]), rolled)
    xs = c + mu_ref[...] * (prev - c)

    rr, k, vv = xs[:, 0:w], xs[:, w:2 * w], xs[:, 2 * w:3 * w]
    lora = xs[:, 3 * w:3 * w + RW_LORA_W + RW_LORA_A]
    gl = xs[:, 3 * w + RW_LORA_W + RW_LORA_A:]
    w0, a0, k_k, k_a, r_k = (vec_ref[i:i + 1, :] for i in range(5))
    lw = -math.exp(-0.5) * _sigmoid(w0 + _dg(jnp.tanh(lora), wup_ref[...]))
    a = _sigmoid(a0 + _dg(lora, aup_ref[...]))
    g_ref[...] = _dg(_sigmoid(gl), gup_ref[...])
    bd = bd_ref[...]
    kkr = k * k_k
    kk = kkr / jnp.maximum(jnp.sqrt(_group_sum_wide(kkr * kkr, bd)), 1e-12)
    kx = k * (1.0 + (a - 1.0) * k_a)
    bn_ref[...] = _group_sum_wide(rr * kx * r_k, bd) * vv
    bb = kk * a
    aa = -kk

    d = _dot_sel(w_ref[...], lw)
    dm, gin, suf = d[0:rows], d[rows:2 * rows], d[2 * rows:3 * rows]
    e_inv = jnp.exp(-dm)
    e_abs = jnp.exp(gin)
    e_suf = jnp.exp(suf)
    a_mid = aa * jnp.exp(dm - lw)
    r_mid = rr * jnp.exp(dm)
    b_inv = bb * e_inv
    k_inv = kx * e_inv
    a_abs = aa * jnp.exp(gin - lw)
    r_abs = rr * e_abs
    b_end = bb * e_suf
    k_end = kx * e_suf

    lane = lax.broadcasted_iota(jnp.int32, (rows, LANES), 1)
    m0 = (lane < RW_HEAD).astype(F32)
    m1 = 1.0 - m0
    masks = [m_ref[i] for i in range(3 + nlev)]
    n2 = 2 * rows
    pairs = range(npair)
    s2 = lambda x, p: _stack2(x[:, p * LANES:(p + 1) * LANES], m0, m1)

    big = [_bdot(jnp.concatenate([s2(a_mid, p), s2(r_mid, p)], axis=0),
                 jnp.concatenate([s2(b_inv, p), s2(k_inv, p)], axis=0), NT) for p in pairs]
    nmat = [big[p][0:n2, 0:n2] * masks[0] for p in pairs]
    tinv = [masks[2] + nmat[p] * masks[3] for p in pairs]
    for l in range(2, nlev + 1):
        tinv = [tinv[p] + _bdot(_bdot(tinv[p], nmat[p] * masks[2 + l]), tinv[p]) for p in pairs]
    v_st = [s2(vv, p) for p in pairs]
    cmask = jnp.concatenate([masks[0], masks[1]], axis=0)
    kv = [_bdot(big[p][:, n2:] * cmask, v_st[p]) for p in pairs]
    pq = [_bdot(tinv[p], jnp.concatenate([s2(a_abs, p), kv[p][0:n2]], axis=1)) for p in pairs]
    ry = [_bdot(big[p][n2:, 0:n2] * masks[1], pq[p]) for p in pairs]
    eye2 = masks[2][0:RW_HEAD, 0:LANES] + masks[2][RW_HEAD:LANES, 0:LANES]
    for p in pairs:
        sl = slice(p * LANES, (p + 1) * LANES)
        r2 = s2(r_abs, p) + ry[p][:, 0:LANES]
        y2 = ry[p][:, LANES:] + kv[p][n2:]
        r2_ref[:, sl] = r2[0:rows] + r2[rows:]
        y2_ref[:, sl] = y2[0:rows] + y2[rows:]
        b_st = s2(b_end, p)
        k_st = s2(k_end, p)
        for b in range(nseq):
            pick = lambda x: jnp.concatenate([x[b * seq:(b + 1) * seq], x[rows + b * seq:rows + (b + 1) * seq]], axis=0)
            mz = _bdot(pick(pq[p]), pick(b_st), TN)
            zz = mz[LANES:] + _bdot(pick(v_st[p]), pick(k_st), TN)
            g_end = e_abs[b * seq + seq - 1:b * seq + seq, sl]
            hs = slice(b * RW_HEAD, (b + 1) * RW_HEAD)
            mx_ref[hs, sl] = mz[0:RW_HEAD] + mz[RW_HEAD:LANES] + eye2 * g_end
            z_ref[hs, sl] = zz[0:RW_HEAD] + zz[RW_HEAD:]


def rwkv_chunk(c_rw, shift0, lp, batch, seq):
    rows_all, ncol = c_rw.shape
    w = lp['rw_w0'].shape[0]
    rows = CHUNK
    cseq = min(seq, rows)
    nseq = rows // cseq
    nchunk = seq // cseq
    wmat, masks, first = _rwkv_consts(rows, cseq)
    vecs = jnp.stack([lp['rw_w0'], lp['rw_a0'], lp['rw_k_k'], lp['rw_k_a'], lp['rw_r_k'].reshape(w)])
    wup = jnp.concatenate([lp['rw_w_up'], jnp.zeros_like(lp['rw_a_up'])], axis=0)
    aup = jnp.concatenate([jnp.zeros_like(lp['rw_w_up']), lp['rw_a_up']], axis=0)
    bd = jnp.asarray(_head_blockdiag(LANES, RW_HEAD), BF16)
    mu2 = lp['rw_mu'].reshape(1, ncol)
    if nseq == 1:
        sh = shift0.reshape(batch, 1, ncol)
        sh_spec = pl.BlockSpec((None, 1, ncol), lambda i: (i // nchunk, 0, 0))
    else:
        sh = shift0
        sh_spec = pl.BlockSpec((nseq, ncol), lambda i: (i, 0))
    consts = [mu2, vecs, wup, aup, lp['rw_g_up'], bd, jnp.asarray(wmat, BF16), jnp.asarray(masks),
              jnp.asarray(first, BF16)]
    full = lambda a: pl.BlockSpec(a.shape, lambda i: (0,) * a.ndim)
    row = pl.BlockSpec((rows, w), lambda i: (i, 0))
    st = pl.BlockSpec((nseq * RW_HEAD, w), lambda i: (i, 0))
    nst = rows_all // cseq * RW_HEAD
    pblk = SUBLANES * (4 // c_rw.dtype.itemsize)
    per = rows // pblk
    return pl.pallas_call(
        functools.partial(_rwkv_chunk_body, seq=cseq, nchunk=nchunk),
        grid=(rows_all // rows,),
        in_specs=[pl.BlockSpec((rows, ncol), lambda i: (i, 0)),
                  pl.BlockSpec((pblk, ncol), lambda i: (jnp.maximum(i * per - 1, 0), 0)),
                  sh_spec] + [full(a) for a in consts],
        out_specs=[row, row, row, row, st, st],
        out_shape=[jax.ShapeDtypeStruct((rows_all, w), F32)] * 4 + [jax.ShapeDtypeStruct((nst, w), F32)] * 2,
        compiler_params=_params("parallel"),
        name="rwkv_chunk",
    )(c_rw, c_rw, sh, *consts)


def _rwkv_scan_body(r2_ref, y2_ref, mx_ref, z_ref, s0_ref, bn_ref, g_ref, vec_ref, bd_ref,
                    o_ref, s_ref, st_ref, *, seq):
    c = pl.program_id(1)
    rows = r2_ref.shape[0]
    nseq = rows // seq
    w = r2_ref.shape[1]

    @pl.when(c == 0)
    def _():
        st_ref[...] = s0_ref[...]

    lane = lax.broadcasted_iota(jnp.int32, (RW_HEAD, LANES), 1)
    m0 = (lane < RW_HEAD).astype(F32)
    m1 = 1.0 - m0
    r2, y2 = r2_ref[...], y2_ref[...]
    ys = []
    for b in range(nseq):
        rs = slice(b * seq, (b + 1) * seq)
        hs = slice(b * RW_HEAD, (b + 1) * RW_HEAD)
        yb = []
        for p in range(w // LANES):
            sl = slice(p * LANES, (p + 1) * LANES)
            s = st_ref[b, :, sl]
            yb.append(_dg(r2[rs, sl], _stack2(s, m0, m1), NT) + y2[rs, sl])
            st_ref[b, :, sl] = _dg(s, _stack2(mx_ref[hs, sl], m0, m1), NN) + z_ref[hs, sl]
        ys.append(jnp.concatenate(yb, axis=1))
    y = jnp.concatenate(ys, axis=0) if nseq > 1 else ys[0]
    bd = bd_ref[...]
    mean = _group_sum_wide(y, bd) * (1.0 / RW_HEAD)
    yc = y - mean
    var = _group_sum_wide(yc * yc, bd) * (1.0 / RW_HEAD)
    yn = yc * lax.rsqrt(var + RW_LNX_EPS) * vec_ref[0:1, :] + vec_ref[1:2, :]
    o_ref[...] = ((yn + bn_ref[...]) * g_ref[...]).astype(o_ref.dtype)

    @pl.when(c == pl.num_programs(1) - 1)
    def _():
        s_ref[...] = st_ref[...]


def rwkv_scan(r2, y2, mx, z, s0, bonus, g, vecs, batch, seq):
    rows_all, w = r2.shape
    rows = CHUNK
    cseq = min(seq, rows)
    nseq = rows // cseq
    nc = seq // cseq
    bd = jnp.asarray(_head_blockdiag(LANES, RW_HEAD), BF16)
    row = pl.BlockSpec((rows, w), lambda i, c: (i * nc + c, 0))
    st = pl.BlockSpec((nseq * RW_HEAD, w), lambda i, c: (i * nc + c, 0))
    sblk = pl.BlockSpec((nseq, RW_HEAD, w), lambda i, c: (i, 0, 0))
    return pl.pallas_call(
        functools.partial(_rwkv_scan_body, seq=cseq),
        grid=(batch // nseq, nc),
        in_specs=[row, row, st, st, sblk, row, row,
                  pl.BlockSpec(vecs.shape, lambda i, c: (0, 0)), pl.BlockSpec(bd.shape, lambda i, c: (0, 0))],
        out_specs=[row, sblk],
        out_shape=[jax.ShapeDtypeStruct((rows_all, w), BF16), jax.ShapeDtypeStruct(s0.shape, F32)],
        scratch_shapes=[pltpu.VMEM((nseq, RW_HEAD, w), F32)],
        compiler_params=_params("parallel", "arbitrary"),
        name="rwkv_scan",
    )(r2, y2, mx, z, s0, bonus, g, vecs, bd)


def rwkv_branch(c_rw, shift0, s0, lp, batch, seq):
    ncol = c_rw.shape[1]
    w = lp['rw_w0'].shape[0]
    nh = w // RW_HEAD
    r2, y2, bonus, g, mx, z = rwkv_chunk(c_rw, shift0, lp, batch, seq)
    s0t = jnp.transpose(s0, (0, 2, 1, 3)).reshape(batch, RW_HEAD, w)
    ovecs = jnp.stack([lp['rw_lnx_w'], lp['rw_lnx_b']])
    out, s_fin = rwkv_scan(r2, y2, mx, z, s0t, bonus, g, ovecs, batch, seq)
    s_fin = jnp.transpose(s_fin.reshape(batch, RW_HEAD, nh, RW_HEAD), (0, 2, 1, 3))
    return out, s_fin, c_rw.reshape(batch, seq, ncol)[:, -1].astype(F32)


def _att_prompt_body(q_ref, kp_ref, kc_ref, vp_ref, vc_ref, o_ref, l_ref, *, sub, dil):
    n = pl.program_id(1)
    scale = ATT_HD ** -0.5
    qi = lax.broadcasted_iota(jnp.int32, (sub, 2 * sub), 0)
    ki = lax.broadcasted_iota(jnp.int32, (sub, 2 * sub), 1) - sub
    dist = qi - ki
    valid = (dist >= 0) & (dist <= sub) & (n * sub + ki >= 0)
    lane = lax.broadcasted_iota(jnp.int32, (1, LANES), 1)
    hm = [(lane < ATT_HD).astype(F32), (lane >= ATT_HD).astype(F32)]
    for r in range(dil):
        rs = pl.ds(r, sub, stride=dil) if dil > 1 else slice(None)
        for p in range(q_ref.shape[1] // LANES):
            sl = slice(p * LANES, (p + 1) * LANES)
            q = q_ref[rs, sl] * scale
            kcat = jnp.concatenate([kp_ref[rs, sl], kc_ref[rs, sl]], axis=0).astype(BF16)
            vcat = jnp.concatenate([vp_ref[rs, sl], vc_ref[rs, sl]], axis=0)
            o = jnp.zeros((sub, LANES), F32)
            lse = jnp.zeros((sub, LANES), F32)
            for m in hm:
                s = _dg((q * m).astype(BF16), kcat, NT)
                s = jnp.where(valid, s, -jnp.inf)
                mx = jnp.max(s, axis=-1, keepdims=True)
                pr = jnp.exp(s - mx)
                den = jnp.sum(pr, axis=-1, keepdims=True)
                o = o + _bdot(pr, vcat * m, NN) / den
                lse = lse + (mx + jnp.log(den)) * m
            o_ref[rs, sl] = o
            l_ref[rs, sl] = lse


def att_prompt_group(c_att, gi, batch, seq):
    win, dil = ATT_GROUPS[gi]
    sub = win // dil
    nblk = seq // win
    assert seq % win == 0
    cw = ATT_WIDTH if dil == 1 else LANES
    ppw = ATT_WIDTH // cw
    cur = lambda j: pl.BlockSpec((win, cw), lambda b, n, p: (b * nblk + n, (3 * gi + j) * ppw + p))
    prv = lambda j: pl.BlockSpec((win, cw),
                                 lambda b, n, p: (b * nblk + jnp.maximum(n - 1, 0), (3 * gi + j) * ppw + p))
    outb = pl.BlockSpec((win, cw), lambda b, n, p: (b * nblk + n, p))
    return pl.pallas_call(
        functools.partial(_att_prompt_body, sub=sub, dil=dil),
        grid=(batch, nblk, ppw),
        in_specs=[cur(0), prv(1), cur(1), prv(2), cur(2)],
        out_specs=[outb, outb],
        out_shape=[jax.ShapeDtypeStruct((batch * seq, ATT_WIDTH), F32)] * 2,
        compiler_params=_params("parallel", "parallel", "parallel"),
        name="att_prompt",
    )(c_att, c_att, c_att, c_att, c_att)


def _att_sample_body(q_ref, kn_ref, vn_ref, c_ref, bias_ref, o_ref, l_ref, *, dil, seq):
    hpq = MXU_DIM // ATT_HD
    trow = lax.broadcasted_iota(jnp.int32, (hpq * seq, seq), 0) % seq
    tcol = lax.broadcasted_iota(jnp.int32, (hpq * seq, seq), 1)
    new_ok = (tcol <= trow) & ((trow - tcol) % dil == 0)
    lane_head = lax.broadcasted_iota(jnp.int32, (seq, MXU_DIM), 1) // ATT_HD
    hm = [(lane_head == h).astype(F32) for h in range(hpq)]
    bias = bias_ref[...]
    for b in range(c_ref.shape[0]):
        rs = slice(b * seq, (b + 1) * seq)
        q = q_ref[rs, :] * (ATT_HD ** -0.5)
        kn, vn = kn_ref[rs, :], vn_ref[rs, :]
        for g in range(ATT_WIDTH // MXU_DIM):
            cs = slice(g * MXU_DIM, (g + 1) * MXU_DIM)
            qst = jnp.concatenate([q[:, cs] * m for m in hm], axis=0)
            sc = _bdot(qst, c_ref[b, 0, cs, :], NN) + bias
            sn = jnp.where(new_ok, _bdot(qst, kn[:, cs], NT), -jnp.inf)
            mx = jnp.maximum(jnp.max(sc, axis=-1, keepdims=True), jnp.max(sn, axis=-1, keepdims=True))
            pc = jnp.exp(sc - mx)
            pn = jnp.exp(sn - mx)
            den = jnp.sum(pc, axis=-1, keepdims=True) + jnp.sum(pn, axis=-1, keepdims=True)
            o = (_bdot(pc, c_ref[b, 1, cs, :], NT) + _bdot(pn, vn[:, cs], NN)) / den
            lse = mx + jnp.log(den)
            o_ref[rs, cs] = sum(o[h * seq:(h + 1) * seq] * hm[h] for h in range(hpq))
            l_ref[rs, cs] = sum(lse[h * seq:(h + 1) * seq] * hm[h] for h in range(hpq))


def att_sample_group(c_att, cache, layer, gi, batch, seq):
    win, dil = ATT_GROUPS[gi]
    depth, _, length = cache.shape[:3]
    assert length == win and seq <= win // dil
    cv = jnp.transpose(cache, (0, 1, 3, 4, 5, 2)).reshape(depth, batch, 2, ATT_WIDTH, length)
    hpq = MXU_DIM // ATT_HD
    t = np.tile(np.arange(seq), hpq)[:, None]
    j = np.arange(length)[None, :]
    bias = jnp.asarray(np.where((j >= t) & ((j - t) % dil == 0), 0.0, -np.inf), F32)
    nb = max(1, min(8, ATT_GROUPS[-1][0] // length))
    assert batch % nb == 0
    new = lambda c: pl.BlockSpec((nb * seq, ATT_WIDTH), lambda b: (b, 3 * gi + c))
    outb = pl.BlockSpec((nb * seq, ATT_WIDTH), lambda b: (b, 0))
    return pl.pallas_call(
        functools.partial(_att_sample_body, dil=dil, seq=seq),
        grid=(batch // nb,),
        in_specs=[new(0), new(1), new(2),
                  pl.BlockSpec((None, nb, 2, ATT_WIDTH, length), lambda b: (layer, b, 0, 0, 0)),
                  pl.BlockSpec(bias.shape, lambda b: (0, 0))],
        out_specs=[outb, outb],
        out_shape=[jax.ShapeDtypeStruct((batch * seq, ATT_WIDTH), F32)] * 2,
        compiler_params=_params("parallel"),
        name="att_sample",
    )(c_att, c_att, c_att, cv, bias)


def _merge_body(x_ref, gate_ref, hg_ref, rw_ref, o1_ref, o2_ref, o3_ref, l1_ref, l2_ref, l3_ref,
                whg_ref, wrw_ref, watt_ref, wout_ref, gn_ref, xo_ref, ho_ref):
    d = x_ref.shape[1]
    l1, l2, l3 = l1_ref[...], l2_ref[...], l3_ref[...]
    mx = jnp.maximum(jnp.maximum(l1, l2), l3)
    e1, e2, e3 = jnp.exp(l1 - mx), jnp.exp(l2 - mx), jnp.exp(l3 - mx)
    att = (e1 * o1_ref[...] + e2 * o2_ref[...] + e3 * o3_ref[...]) / (e1 + e2 + e3)
    gates = _sigmoid(gate_ref[...].astype(F32))
    bm = lambda a, w_ref: jnp.dot(a.astype(BF16), w_ref[...], preferred_element_type=F32)
    merged = (gates[:, 0:d] * bm(hg_ref[...], whg_ref)
              + gates[:, d:2 * d] * bm(rw_ref[...], wrw_ref)
              + gates[:, 2 * d:3 * d] * bm(att, watt_ref))
    xn = x_ref[...] + bm(merged, wout_ref)
    xo_ref[...] = xn
    ho_ref[...] = _rms(xn, gn_ref[...]).astype(ho_ref.dtype)


def merge_branches(x, c_gate, o_hg, o_rw, att_o, att_l, w_hg, w_rw, w_att, w_out, g_next):
    rows, d = x.shape
    tm = _row_tile(rows, 256)
    row = lambda a: pl.BlockSpec((tm, a.shape[1]), lambda i: (i, 0))
    full = lambda a: pl.BlockSpec(a.shape, lambda i: (0, 0))
    g2 = g_next.reshape(1, d)
    args = [x, c_gate, o_hg, o_rw, *att_o, *att_l]
    wts = [w_hg, w_rw, w_att, w_out, g2]
    return pl.pallas_call(
        _merge_body,
        grid=(rows // tm,),
        in_specs=[row(a) for a in args] + [full(a) for a in wts],
        out_specs=[pl.BlockSpec((tm, d), lambda i: (i, 0))] * 2,
        out_shape=[jax.ShapeDtypeStruct((rows, d), F32), jax.ShapeDtypeStruct((rows, d), BF16)],
        compiler_params=_params("parallel"),
        name="merge_branches",
    )(*args, *wts)


def _trunk_layer(x, h, lw, layer, batch, seq, state, last):
    d = x.shape[1]
    x, h = ffn_half_step(x, h, lw['ffn1_w_gu'], lw['ffn1_w_down'], lw['norm_mix'], BF16)
    c_hg = project(h, lw['w_in_hg'], 1024)
    c_rw = project(h, lw['w_in_rw'], lw['w_in_rw'].shape[1] // 2, BF16)
    c_att = project(h, lw['w_in_att'], 768)
    c_gate = project(h, lw['w_in_gate'], 1024, BF16)
    if state is None:
        o_hg, s_hg = hgrn_prompt(c_hg, lw['hg_params'], batch, seq)
        shift0 = jnp.zeros((batch, c_rw.shape[1]), F32)
        s_rw0 = jnp.zeros((batch, d // RW_HEAD, RW_HEAD, RW_HEAD), F32)
    else:
        s_hgrn, s_rwkv, s_shift, caches = state
        o_hg, s_hg = hgrn_sample(c_hg, lw['hg_params'], s_hgrn, layer, batch, seq)
        shift0, s_rw0 = s_shift[layer], s_rwkv[layer]
    o_rw, s_rw, shift = rwkv_branch(c_rw, shift0, s_rw0, lw, batch, seq)
    att_o, att_l, kv_rows = [], [], []
    c_att3 = c_att.reshape(batch, seq, -1)
    for gi, (win, dil) in enumerate(ATT_GROUPS):
        if state is None:
            o, l = att_prompt_group(c_att, gi, batch, seq)
        else:
            o, l = att_sample_group(c_att, caches[gi], layer, gi, batch, seq)
        att_o.append(o)
        att_l.append(l)
        keep = min(win, seq)
        kv = c_att3[:, seq - keep:, (3 * gi + 1) * ATT_WIDTH:(3 * gi + 3) * ATT_WIDTH]
        kv_rows.append(kv.reshape(batch, keep, 2, ATT_HEADS, ATT_HD))
    x, h = merge_branches(x, c_gate, o_hg, o_rw, att_o, att_l, lw['w_branch_hg'], lw['w_branch_rw'],
                          lw['w_branch_att'], lw['w_out'], lw['norm_ffn2'])
    x, h = ffn_half_step(x, h, lw['ffn2_w_gu'], lw['ffn2_w_down'], lw['norm_next'], F32 if last else BF16)
    return x, h, (s_hg, s_rw, shift, *kv_rows)


def _lower_bounds(hg_lb):
    p = jax.nn.softmax(hg_lb.astype(F32), axis=0)
    c = jnp.cumsum(p, axis=0)
    return c - c[0:1]


def kernel(x_prompt, x_sample, state_hgrn, state_rwkv, state_rwkv_shift, cache_att1_kv, cache_att2_kv, cache_att3_kv, norm_ffn1, ffn1_w_gu, ffn1_w_down, norm_mix, w_in, hg_lb, hg_gnorm, rw_mu, rw_w0, rw_w_up, rw_a0, rw_a_up, rw_g_up, rw_k_k, rw_k_a, rw_r_k, rw_lnx_w, rw_lnx_b, w_branch_hg, w_branch_rw, w_branch_att, w_out, norm_ffn2, ffn2_w_gu, ffn2_w_down, norm_final):
    depth = norm_ffn1.shape[0]
    bp, sp, d = x_prompt.shape
    bs, ss, _ = x_sample.shape
    hg_cols = 2 * HG_HEADS * HG_DK + 2 * hg_gnorm.shape[1]
    rw_cols = rw_mu.shape[1]
    att_cols = 3 * len(ATT_GROUPS) * ATT_WIDTH
    bounds = np.cumsum([0, hg_cols, rw_cols, att_cols, 3 * d])
    assert bounds[-1] == w_in.shape[2]
    lbs = _lower_bounds(hg_lb)
    layers = []
    for l in range(depth):
        lb = lbs[l]
        wl = w_in[l].astype(BF16)
        layers.append({
            'ffn1_w_gu': ffn1_w_gu[l].astype(BF16), 'ffn1_w_down': ffn1_w_down[l].astype(BF16),
            'ffn2_w_gu': ffn2_w_gu[l].astype(BF16), 'ffn2_w_down': ffn2_w_down[l].astype(BF16),
            'norm_mix': norm_mix[l], 'norm_ffn2': norm_ffn2[l],
            'norm_next': norm_ffn1[l + 1] if l + 1 < depth else norm_final,
            'w_in_hg': wl[:, bounds[0]:bounds[1]], 'w_in_rw': wl[:, bounds[1]:bounds[2]],
            'w_in_att': wl[:, bounds[2]:bounds[3]], 'w_in_gate': wl[:, bounds[3]:bounds[4]],
            'hg_params': jnp.stack([jnp.log(lb), jnp.log1p(-lb), 1.0 - lb, hg_gnorm[l]]),
            'rw_mu': rw_mu[l], 'rw_w0': rw_w0[l], 'rw_w_up': rw_w_up[l], 'rw_a0': rw_a0[l], 'rw_a_up': rw_a_up[l],
            'rw_g_up': rw_g_up[l], 'rw_k_k': rw_k_k[l], 'rw_k_a': rw_k_a[l], 'rw_r_k': rw_r_k[l],
            'rw_lnx_w': rw_lnx_w[l], 'rw_lnx_b': rw_lnx_b[l],
            'w_branch_hg': w_branch_hg[l].astype(BF16), 'w_branch_rw': w_branch_rw[l].astype(BF16),
            'w_branch_att': w_branch_att[l].astype(BF16), 'w_out': w_out[l].astype(BF16),
        })
    caches = (cache_att1_kv, cache_att2_kv, cache_att3_kv)
    sample_state = (state_hgrn, state_rwkv, state_rwkv_shift, caches)
    results = []
    for x3, batch, seq, state in ((x_prompt, bp, sp, None), (x_sample, bs, ss, sample_state)):
        x = x3.reshape(batch * seq, d)
        h = rmsnorm_rows(x, norm_ffn1[0], BF16)
        states = []
        for l in range(depth):
            x, h, st = _trunk_layer(x, h, layers[l], l, batch, seq, state, l + 1 == depth)
            states.append(st)
        stacked = [jnp.stack(z, axis=0) for z in zip(*states)]
        results.append((h.reshape(batch, seq, d), stacked))
    (yp, pst), (ys, sst) = results
    return (yp, ys, *pst, *sst)
```

```python
import functools
import math

import numpy as np
import jax
import jax.numpy as jnp
from jax import lax
from jax.experimental import pallas as pl
from jax.experimental.pallas import tpu as pltpu

F32 = jnp.float32
BF16 = jnp.bfloat16

HG_HEADS = 8
HG_DK = 128
RW_HEAD = 64
RW_LORA_W = 64
RW_LORA_A = 64
RW_LORA_G = 128
RW_LNX_EPS = 64e-5
ATT_GROUPS = ((128, 1), (512, 4), (2048, 16))
ATT_HEADS = 8
ATT_HD = 64
ATT_WIDTH = ATT_HEADS * ATT_HD
RMS_EPS = 1e-6
LANES = 128
SUBLANES = 8
MXU_DIM = 256
CHUNK = 64
HG_CHUNK = 128
VMEM_LIMIT = 48 * 1024 * 1024

NN = (((1,), (0,)), ((), ()))
NT = (((1,), (1,)), ((), ()))
TN = (((0,), (0,)), ((), ()))


def _dg(a, b, dims=NN):
    return lax.dot_general(a, b, dims, preferred_element_type=F32)


def _bdot(a, b, dims=NN):
    return lax.dot_general(a.astype(BF16), b.astype(BF16), dims, preferred_element_type=F32)


def _split2(x):
    hi = x.astype(BF16)
    lo = (x - hi.astype(F32)).astype(BF16)
    return hi, lo


def _split3(x):
    hi = x.astype(BF16)
    r = x - hi.astype(F32)
    mid = r.astype(BF16)
    lo = (r - mid.astype(F32)).astype(BF16)
    return hi, mid, lo


def _dot_sel(w, x, dims=NN):
    hi, mid, lo = _split3(x)
    return _dg(w, hi, dims) + _dg(w, mid, dims) + _dg(w, lo, dims)


def _dot_sel3(w3, x):
    return _dg(w3, jnp.concatenate(_split3(x), axis=0), NN)


def _dot_sel_rhs(x, w, dims=NN):
    hi, mid, lo = _split3(x)
    return _dg(hi, w, dims) + _dg(mid, w, dims) + _dg(lo, w, dims)


def _rms(x, g):
    return x * lax.rsqrt(jnp.mean(x * x, axis=-1, keepdims=True) + RMS_EPS) * g


def _sigmoid(x):
    return 1.0 / (1.0 + jnp.exp(-x))


def _params(*sem):
    return pltpu.CompilerParams(dimension_semantics=sem, vmem_limit_bytes=VMEM_LIMIT)


def _row_tile(rows, want):
    t = min(rows, want)
    assert rows % t == 0
    return t


def _rmsnorm_body(x_ref, g_ref, o_ref):
    o_ref[...] = _rms(x_ref[...], g_ref[...]).astype(o_ref.dtype)


def rmsnorm_rows(x, g, out_dtype):
    rows, d = x.shape
    tm = _row_tile(rows, 1024)
    return pl.pallas_call(
        _rmsnorm_body,
        grid=(rows // tm,),
        in_specs=[pl.BlockSpec((tm, d), lambda i: (i, 0)), pl.BlockSpec((1, d), lambda i: (0, 0))],
        out_specs=pl.BlockSpec((tm, d), lambda i: (i, 0)),
        out_shape=jax.ShapeDtypeStruct((rows, d), out_dtype),
        compiler_params=_params("parallel"),
        name="rmsnorm",
    )(x, g.reshape(1, d))


def _ffn_body(x_ref, h_ref, wg_ref, wu_ref, wd_ref, gn_ref, xo_ref, ho_ref, acc_ref):
    j = pl.program_id(1)

    @pl.when(j == 0)
    def _():
        acc_ref[...] = jnp.zeros_like(acc_ref)

    h = h_ref[...]
    gate = jnp.dot(h, wg_ref[...], preferred_element_type=F32)
    up = jnp.dot(h, wu_ref[...], preferred_element_type=F32)
    act = (gate * _sigmoid(gate) * up).astype(BF16)
    acc_ref[...] += jnp.dot(act, wd_ref[...], preferred_element_type=F32)

    @pl.when(j == pl.num_programs(1) - 1)
    def _():
        xn = x_ref[...] + 0.5 * acc_ref[...]
        xo_ref[...] = xn
        ho_ref[...] = _rms(xn, gn_ref[...]).astype(ho_ref.dtype)


def ffn_half_step(x, h, w_gu, w_down, g_next, next_dtype):
    rows, d = x.shape
    dff = w_down.shape[0]
    tm = _row_tile(rows, 1024)
    tf = 256
    nf = dff // tf
    assert dff % tf == 0
    return pl.pallas_call(
        _ffn_body,
        grid=(rows // tm, nf),
        in_specs=[
            pl.BlockSpec((tm, d), lambda i, j: (i, 0)),
            pl.BlockSpec((tm, d), lambda i, j: (i, 0)),
            pl.BlockSpec((d, tf), lambda i, j: (0, j)),
            pl.BlockSpec((d, tf), lambda i, j: (0, j + nf)),
            pl.BlockSpec((tf, d), lambda i, j: (j, 0)),
            pl.BlockSpec((1, d), lambda i, j: (0, 0)),
        ],
        out_specs=[pl.BlockSpec((tm, d), lambda i, j: (i, 0)), pl.BlockSpec((tm, d), lambda i, j: (i, 0))],
        out_shape=[jax.ShapeDtypeStruct((rows, d), F32), jax.ShapeDtypeStruct((rows, d), next_dtype)],
        scratch_shapes=[pltpu.VMEM((tm, d), F32)],
        compiler_params=_params("parallel", "arbitrary"),
        name="ffn_half_step",
    )(x, h, w_gu, w_gu, w_down, g_next.reshape(1, d))


def _proj_body(h_ref, w_ref, o_ref):
    o_ref[...] = jnp.dot(h_ref[...], w_ref[...], preferred_element_type=F32).astype(o_ref.dtype)


def project(h, w, tn, out_dtype=F32):
    rows, d = h.shape
    n = w.shape[1]
    tm = _row_tile(rows, 2048)
    assert n % tn == 0
    return pl.pallas_call(
        _proj_body,
        grid=(rows // tm, n // tn),
        in_specs=[pl.BlockSpec((tm, d), lambda i, j: (i, 0)), pl.BlockSpec((d, tn), lambda i, j: (0, j))],
        out_specs=pl.BlockSpec((tm, tn), lambda i, j: (i, j)),
        out_shape=jax.ShapeDtypeStruct((rows, n), out_dtype),
        compiler_params=_params("parallel", "parallel"),
        name="project",
    )(h, w)


def _level_constants(rows, seq):
    levels = int(math.log2(seq))
    assert 2 ** levels == seq and rows % seq == 0
    t = np.arange(rows)
    r = t[None, :]
    ws, ms = [], [np.eye(rows, dtype=bool)]
    for l in range(1, levels + 1):
        m = 2 ** l
        hm = m // 2
        par, pos = t // m, t % m
        sec = pos >= hm
        ref = par * m + hm - 1
        w = np.where(sec[:, None], (r > ref[:, None]) & (r <= t[:, None]), (r > t[:, None]) & (r <= ref[:, None]))
        ws.append(w)
        ms.append((par[:, None] == par[None, :]) & sec[:, None] & (~sec)[None, :])
    same = (t // seq)[:, None] == (t // seq)[None, :]
    ws.append(same & (r <= t[:, None]))
    ws.append(same & (r > t[:, None]))
    wmat = np.concatenate(ws, 0).astype(np.float32)
    return np.concatenate([wmat] * 3, axis=1), np.stack(ms).astype(np.float32)


def _hgrn_gates(f, loglb, log1mlb, onemlb):
    ls = jnp.minimum(f, 0.0) - jnp.log(1.0 + jnp.exp(-jnp.abs(f)))
    a = log1mlb + ls
    log_f = jnp.maximum(a, loglb) + jnp.log(1.0 + jnp.exp(-jnp.abs(a - loglb)))
    key = onemlb * jnp.exp(ls - f)
    return -log_f, key


def _hgrn_intra(q, k, e_lv, masks, nlev):
    a = _dg(q, k, NT) * masks[0]
    for l in range(1, nlev + 1):
        e = e_lv[l - 1]
        a = a + _dg(q * e, k * e, NT) * masks[l]
    return a


def _hgrn_prompt_body(q_ref, f_ref, i_ref, g_ref, lb_ref, w_ref, m_ref, o_ref, s_ref, st_ref, *, nlev):
    c = pl.program_id(1)
    rows = q_ref.shape[0]

    @pl.when(c == 0)
    def _():
        st_ref[...] = jnp.zeros_like(st_ref)

    nlf, key = _hgrn_gates(f_ref[...], lb_ref[0:1, :], lb_ref[1:2, :], lb_ref[2:3, :])
    dec = jnp.exp(-_dot_sel3(w_ref[...], nlf))
    qraw = q_ref[...]
    qact = qraw * _sigmoid(qraw)
    val = i_ref[...]
    og = g_ref[...]
    masks = [m_ref[l] for l in range(nlev + 1)]
    for h in range(HG_HEADS):
        sl = slice(h * HG_DK, (h + 1) * HG_DK)
        q, k, v = qact[:, sl], key[:, sl], val[:, sl]
        e_lv = [dec[l * rows:(l + 1) * rows, sl] for l in range(nlev)]
        e_cum = dec[nlev * rows:(nlev + 1) * rows, sl]
        e_suf = dec[(nlev + 1) * rows:(nlev + 2) * rows, sl]
        a = _hgrn_intra(q, k, e_lv, masks, nlev)
        st = st_ref[h]
        o = _dg(q * e_cum, st, NT) + _dg(a, v, NN)
        st_new = st * e_cum[rows - 1:rows, :] + _dg(v, k * e_suf, TN)
        st_ref[h] = st_new
        y = _rms(o, lb_ref[3:4, sl])
        ogh = og[:, sl]
        o_ref[:, sl] = (y * (ogh * _sigmoid(ogh))).astype(o_ref.dtype)

    @pl.when(c == pl.num_programs(1) - 1)
    def _():
        for h in range(HG_HEADS):
            s_ref[0, h] = st_ref[h].T


def hgrn_prompt(c_hg, lbp, batch, seq):
    width = c_hg.shape[1] // 4
    rows = min(seq, HG_CHUNK)
    nlev = int(math.log2(rows))
    wmat, masks = _level_constants(rows, rows)
    nc = seq // rows
    col = lambda j: pl.BlockSpec((rows, width), lambda b, c: (b * nc + c, j))
    return pl.pallas_call(
        functools.partial(_hgrn_prompt_body, nlev=nlev),
        grid=(batch, nc),
        in_specs=[col(0), col(1), col(2), col(3),
                  pl.BlockSpec(lbp.shape, lambda b, c: (0, 0)),
                  pl.BlockSpec(wmat.shape, lambda b, c: (0, 0)),
                  pl.BlockSpec(masks.shape, lambda b, c: (0, 0, 0))],
        out_specs=[pl.BlockSpec((rows, width), lambda b, c: (b * nc + c, 0)),
                   pl.BlockSpec((1, HG_HEADS, HG_DK, HG_DK), lambda b, c: (b, 0, 0, 0))],
        out_shape=[jax.ShapeDtypeStruct((batch * seq, width), BF16),
                   jax.ShapeDtypeStruct((batch, HG_HEADS, HG_DK, HG_DK), F32)],
        scratch_shapes=[pltpu.VMEM((HG_HEADS, HG_DK, HG_DK), F32)],
        compiler_params=_params("parallel", "arbitrary"),
        name="hgrn_prompt",
    )(c_hg, c_hg, c_hg, c_hg, lbp, jnp.asarray(wmat, BF16), jnp.asarray(masks))


def _hgrn_sample_body(q_ref, f_ref, i_ref, g_ref, lb_ref, w_ref, m_ref, s0_ref, o_ref, s_ref, *, nlev, seq):
    rows = q_ref.shape[0]
    nseq = rows // seq
    nlf, key = _hgrn_gates(f_ref[...], lb_ref[0:1, :], lb_ref[1:2, :], lb_ref[2:3, :])
    dec = jnp.exp(-_dot_sel3(w_ref[...], nlf))
    qraw = q_ref[...]
    q = qraw * _sigmoid(qraw)
    v = i_ref[...]
    og = g_ref[...]
    masks = [m_ref[l] for l in range(nlev + 1)]
    e_lv = [dec[l * rows:(l + 1) * rows] for l in range(nlev)]
    e_cum = dec[nlev * rows:(nlev + 1) * rows]
    e_suf = dec[(nlev + 1) * rows:(nlev + 2) * rows]
    a = _hgrn_intra(q, key, e_lv, masks, nlev)
    o_intra = _dg(a, v, NN)
    qc = q * e_cum
    ks = key * e_suf
    ones = jnp.ones((seq, HG_DK), BF16)
    outs = []
    for b in range(nseq):
        rs = slice(b * seq, (b + 1) * seq)
        s0 = s0_ref[b, 0]
        outs.append(_dg(qc[rs], s0, NN))
        total = _dot_sel_rhs(nlf[rs], ones, TN)
        s_ref[b, 0] = jnp.exp(-total) * s0 + _dg(ks[rs], v[rs], TN)
    o = o_intra + jnp.concatenate(outs, axis=0)
    y = _rms(o, lb_ref[3:4, :])
    o_ref[...] = (y * (og * _sigmoid(og))).astype(o_ref.dtype)


def hgrn_sample(c_hg, lbp, state, layer, batch, seq):
    width = c_hg.shape[1] // 4
    nh = width // HG_DK
    rows = CHUNK
    nseq = rows // seq
    nlev = int(math.log2(seq))
    wmat, masks = _level_constants(rows, seq)
    col = lambda j: pl.BlockSpec((rows, HG_DK), lambda i, h: (i, j * nh + h))
    lbspec = pl.BlockSpec((lbp.shape[0], HG_DK), lambda i, h: (0, h))
    return pl.pallas_call(
        functools.partial(_hgrn_sample_body, nlev=nlev, seq=seq),
        grid=(batch // nseq, nh),
        in_specs=[col(0), col(1), col(2), col(3), lbspec,
                  pl.BlockSpec(wmat.shape, lambda i, h: (0, 0)),
                  pl.BlockSpec(masks.shape, lambda i, h: (0, 0, 0)),
                  pl.BlockSpec((None, nseq, 1, HG_DK, HG_DK), lambda i, h: (layer, i, h, 0, 0))],
        out_specs=[pl.BlockSpec((rows, HG_DK), lambda i, h: (i, h)),
                   pl.BlockSpec((nseq, 1, HG_DK, HG_DK), lambda i, h: (i, h, 0, 0))],
        out_shape=[jax.ShapeDtypeStruct((batch * seq, width), BF16),
                   jax.ShapeDtypeStruct((batch, nh, HG_DK, HG_DK), F32)],
        compiler_params=_params("parallel", "parallel"),
        name="hgrn_sample",
    )(c_hg, c_hg, c_hg, c_hg, lbp, jnp.asarray(wmat, BF16), jnp.asarray(masks), state)


def _group_sum(x, bd):
    return _dg(jnp.concatenate(_split2(x), axis=1), bd, NN)


def _group_sum_wide(x, bd):
    return jnp.concatenate([_group_sum(x[:, j * LANES:(j + 1) * LANES], bd) for j in range(x.shape[1] // LANES)],
                           axis=1)


def _head_blockdiag(n, head):
    i = np.arange(n) // head
    bd = (i[:, None] == i[None, :]).astype(np.float32)
    return np.concatenate([bd, bd], axis=0)


def _rwkv_consts(rows, seq):
    t = np.arange(rows)
    sq = t // seq
    same = sq[:, None] == sq[None, :]
    r = t[None, :]
    pre = same & (r <= t[:, None])
    mid = sq * seq + seq // 2 - 1
    premid = same & (r <= mid[:, None])
    suf = same & (r > t[:, None])
    wmat = np.concatenate([pre.astype(np.float32) - premid.astype(np.float32), pre, suf], 0)
    t2 = np.concatenate([t, t])
    h2 = np.concatenate([0 * t, 0 * t + 1])
    s2 = t2 // seq
    blk = (h2[:, None] == h2[None, :]) & (s2[:, None] == s2[None, :])
    strict = blk & (t2[None, :] < t2[:, None])
    incl = blk & (t2[None, :] <= t2[:, None])
    lv = []
    for l in range(1, int(math.log2(seq)) + 1):
        m = 2 ** l
        par, pos = t2 // m, t2 % m
        lv.append(blk & (par[:, None] == par[None, :]) & (pos[:, None] >= m // 2) & (pos[None, :] < m // 2))
    masks = np.stack([strict, incl, np.eye(2 * rows, dtype=bool)] + lv).astype(np.float32)
    first = (t[:, None] == (np.arange(rows // seq) * seq)[None, :]).astype(np.float32)
    return np.concatenate([wmat] * 3, axis=1), masks, first


def _stack2(x, m0, m1):
    return jnp.concatenate([x * m0, x * m1], axis=0)


def _rwkv_chunk_body(c_ref, cp_ref, sh_ref, mu_ref, vec_ref, wup_ref, aup_ref, gup_ref, bd_ref, w_ref, m_ref,
                     f_ref, r2_ref, y2_ref, bn_ref, g_ref, mx_ref, z_ref, *, seq, nchunk):
    rows = c_ref.shape[0]
    w = r2_ref.shape[1]
    npair = w // LANES
    nseq = rows // seq
    nlev = int(math.log2(seq))

    c = c_ref[...].astype(F32)
    rolled = pltpu.roll(c, 1, 0)
    rowi = lax.broadcasted_iota(jnp.int32, (rows, 1), 0)
    if nseq == 1:
        last = cp_ref.shape[0] - 1
        head_row = jnp.where(pl.program_id(0) % nchunk == 0, sh_ref[...], cp_ref[last:last + 1, :].astype(F32))
        prev = jnp.where(rowi == 0, head_row, rolled)
    else:
        prev = jnp.where(rowi % seq == 0, _dot_sel(f_ref[...], sh_ref[...]), rolled)
    xs = c + mu_ref[...] * (prev - c)

    rr, k, vv = xs[:, 0:w], xs[:, w:2 * w], xs[:, 2 * w:3 * w]
    lora = xs[:, 3 * w:3 * w + RW_LORA_W + RW_LORA_A]
    gl = xs[:, 3 * w + RW_LORA_W + RW_LORA_A:]
    w0, a0, k_k, k_a, r_k = (vec_ref[i:i + 1, :] for i in range(5))
    lw = -math.exp(-0.5) * _sigmoid(w0 + _dg(jnp.tanh(lora), wup_ref[...]))
    a = _sigmoid(a0 + _dg(lora, aup_ref[...]))
    g_ref[...] = _dg(_sigmoid(gl), gup_ref[...]).astype(g_ref.dtype)
    bd = bd_ref[...]
    kkr = k * k_k
    kk = kkr * lax.rsqrt(jnp.maximum(_group_sum_wide(kkr * kkr, bd), 1e-24))
    kx = k * (1.0 + (a - 1.0) * k_a)
    bn_ref[...] = (_group_sum_wide(rr * kx * r_k, bd) * vv).astype(bn_ref.dtype)
    bb = kk * a
    aa = -kk

    d = _dot_sel3(w_ref[...], lw)
    dm, gin, suf = d[0:rows], d[rows:2 * rows], d[2 * rows:3 * rows]
    e_inv = jnp.exp(-dm)
    e_abs = jnp.exp(gin)
    e_suf = jnp.exp(suf)
    a_mid = aa * jnp.exp(dm - lw)
    r_mid = rr * jnp.exp(dm)
    b_inv = bb * e_inv
    k_inv = kx * e_inv
    a_abs = aa * jnp.exp(gin - lw)
    r_abs = rr * e_abs
    b_end = bb * e_suf
    k_end = kx * e_suf

    lane = lax.broadcasted_iota(jnp.int32, (rows, LANES), 1)
    m0 = (lane < RW_HEAD).astype(F32)
    m1 = 1.0 - m0
    masks = [m_ref[i] for i in range(3 + nlev)]
    n2 = 2 * rows
    pairs = range(npair)
    s2 = lambda x, p: _stack2(x[:, p * LANES:(p + 1) * LANES], m0, m1)

    big = [_bdot(jnp.concatenate([s2(a_mid, p), s2(r_mid, p)], axis=0),
                 jnp.concatenate([s2(b_inv, p), s2(k_inv, p)], axis=0), NT) for p in pairs]
    nmat = [big[p][0:n2, 0:n2] * masks[0] for p in pairs]
    tinv = [masks[2] + nmat[p] * masks[3] for p in pairs]
    for l in range(2, nlev + 1):
        tinv = [tinv[p] + _bdot(_bdot(tinv[p], nmat[p] * masks[2 + l]), tinv[p]) for p in pairs]
    v_st = [s2(vv, p) for p in pairs]
    cmask = jnp.concatenate([masks[0], masks[1]], axis=0)
    kv = [_bdot(big[p][:, n2:] * cmask, v_st[p]) for p in pairs]
    pq = [_bdot(tinv[p], jnp.concatenate([s2(a_abs, p), kv[p][0:n2]], axis=1)) for p in pairs]
    ry = [_bdot(big[p][n2:, 0:n2] * masks[1], pq[p]) for p in pairs]
    eye2 = masks[2][0:RW_HEAD, 0:LANES] + masks[2][RW_HEAD:LANES, 0:LANES]
    for p in pairs:
        sl = slice(p * LANES, (p + 1) * LANES)
        r2 = s2(r_abs, p) + ry[p][:, 0:LANES]
        y2 = ry[p][:, LANES:] + kv[p][n2:]
        r2_ref[:, sl] = (r2[0:rows] + r2[rows:]).astype(r2_ref.dtype)
        y2_ref[:, sl] = (y2[0:rows] + y2[rows:]).astype(y2_ref.dtype)
        b_st = s2(b_end, p)
        k_st = s2(k_end, p)
        for b in range(nseq):
            pick = lambda x: jnp.concatenate([x[b * seq:(b + 1) * seq], x[rows + b * seq:rows + (b + 1) * seq]], axis=0)
            mz = _bdot(pick(pq[p]), pick(b_st), TN)
            zz = mz[LANES:] + _bdot(pick(v_st[p]), pick(k_st), TN)
            g_end = e_abs[b * seq + seq - 1:b * seq + seq, sl]
            hs = slice(b * RW_HEAD, (b + 1) * RW_HEAD)
            mx_ref[hs, sl] = (mz[0:RW_HEAD] + mz[RW_HEAD:LANES] + eye2 * g_end).astype(mx_ref.dtype)
            z_ref[hs, sl] = (zz[0:RW_HEAD] + zz[RW_HEAD:]).astype(z_ref.dtype)


def rwkv_chunk(c_rw, shift0, lp, batch, seq):
    rows_all, ncol = c_rw.shape
    w = lp['rw_w0'].shape[0]
    rows = CHUNK
    cseq = min(seq, rows)
    nseq = rows // cseq
    nchunk = seq // cseq
    wmat, masks, first = _rwkv_consts(rows, cseq)
    vecs = jnp.stack([lp['rw_w0'], lp['rw_a0'], lp['rw_k_k'], lp['rw_k_a'], lp['rw_r_k'].reshape(w)])
    wup = jnp.concatenate([lp['rw_w_up'], jnp.zeros_like(lp['rw_a_up'])], axis=0)
    aup = jnp.concatenate([jnp.zeros_like(lp['rw_w_up']), lp['rw_a_up']], axis=0)
    bd = jnp.asarray(_head_blockdiag(LANES, RW_HEAD), BF16)
    mu2 = lp['rw_mu'].reshape(1, ncol)
    if nseq == 1:
        sh = shift0.reshape(batch, 1, ncol)
        sh_spec = pl.BlockSpec((None, 1, ncol), lambda i: (i // nchunk, 0, 0))
    else:
        sh = shift0
        sh_spec = pl.BlockSpec((nseq, ncol), lambda i: (i, 0))
    consts = [mu2, vecs, wup, aup, lp['rw_g_up'], bd, jnp.asarray(wmat, BF16), jnp.asarray(masks),
              jnp.asarray(first, BF16)]
    full = lambda a: pl.BlockSpec(a.shape, lambda i: (0,) * a.ndim)
    row = pl.BlockSpec((rows, w), lambda i: (i, 0))
    st = pl.BlockSpec((nseq * RW_HEAD, w), lambda i: (i, 0))
    nst = rows_all // cseq * RW_HEAD
    pblk = SUBLANES * (4 // c_rw.dtype.itemsize)
    per = rows // pblk
    return pl.pallas_call(
        functools.partial(_rwkv_chunk_body, seq=cseq, nchunk=nchunk),
        grid=(rows_all // rows,),
        in_specs=[pl.BlockSpec((rows, ncol), lambda i: (i, 0)),
                  pl.BlockSpec((pblk, ncol), lambda i: (jnp.maximum(i * per - 1, 0), 0)),
                  sh_spec] + [full(a) for a in consts],
        out_specs=[row, row, row, row, st, st],
        out_shape=[jax.ShapeDtypeStruct((rows_all, w), BF16)] * 4 + [jax.ShapeDtypeStruct((nst, w), BF16)] * 2,
        compiler_params=_params("parallel"),
        name="rwkv_chunk",
    )(c_rw, c_rw, sh, *consts)


def _rwkv_scan_body(r2_ref, y2_ref, mx_ref, z_ref, s0_ref, bn_ref, g_ref, vec_ref, bd_ref,
                    o_ref, s_ref, st_ref, *, seq):
    c = pl.program_id(1)
    rows = r2_ref.shape[0]
    nunit = rows // seq
    nstate = st_ref.shape[0]
    w = r2_ref.shape[1]

    @pl.when(c == 0)
    def _():
        st_ref[...] = s0_ref[...]

    lane = lax.broadcasted_iota(jnp.int32, (RW_HEAD, LANES), 1)
    m0 = (lane < RW_HEAD).astype(F32)
    m1 = 1.0 - m0
    ys = []
    for u in range(nunit):
        rs = slice(u * seq, (u + 1) * seq)
        hs = slice(u * RW_HEAD, (u + 1) * RW_HEAD)
        b = u if nstate > 1 else 0
        yb = []
        for p in range(w // LANES):
            sl = slice(p * LANES, (p + 1) * LANES)
            s = st_ref[b, :, sl]
            yb.append(_bdot(r2_ref[rs, sl], _stack2(s, m0, m1), NT) + y2_ref[rs, sl].astype(F32))
            st_ref[b, :, sl] = (_bdot(s, _stack2(mx_ref[hs, sl].astype(F32), m0, m1), NN)
                                + z_ref[hs, sl].astype(F32))
        ys.append(jnp.concatenate(yb, axis=1))
    y = jnp.concatenate(ys, axis=0) if nunit > 1 else ys[0]
    bd = bd_ref[...]
    mean = _group_sum_wide(y, bd) * (1.0 / RW_HEAD)
    yc = y - mean
    var = _group_sum_wide(yc * yc, bd) * (1.0 / RW_HEAD)
    yn = yc * lax.rsqrt(var + RW_LNX_EPS) * vec_ref[0:1, :] + vec_ref[1:2, :]
    o_ref[...] = ((yn + bn_ref[...]) * g_ref[...]).astype(o_ref.dtype)

    @pl.when(c == pl.num_programs(1) - 1)
    def _():
        s_ref[...] = st_ref[...]


def rwkv_scan(r2, y2, mx, z, s0, bonus, g, vecs, batch, seq):
    rows_all, w = r2.shape
    cseq = min(seq, CHUNK)
    if seq > CHUNK:
        rows, nseq = min(seq, 2 * CHUNK), 1
    else:
        rows, nseq = CHUNK, CHUNK // cseq
    nunit = rows // cseq
    nc = max(seq // rows, 1)
    bd = jnp.asarray(_head_blockdiag(LANES, RW_HEAD), BF16)
    row = pl.BlockSpec((rows, w), lambda i, c: (i * nc + c, 0))
    st = pl.BlockSpec((nunit * RW_HEAD, w), lambda i, c: (i * nc + c, 0))
    sblk = pl.BlockSpec((nseq, RW_HEAD, w), lambda i, c: (i, 0, 0))
    return pl.pallas_call(
        functools.partial(_rwkv_scan_body, seq=cseq),
        grid=(batch // nseq, nc),
        in_specs=[row, row, st, st, sblk, row, row,
                  pl.BlockSpec(vecs.shape, lambda i, c: (0, 0)), pl.BlockSpec(bd.shape, lambda i, c: (0, 0))],
        out_specs=[row, sblk],
        out_shape=[jax.ShapeDtypeStruct((rows_all, w), BF16), jax.ShapeDtypeStruct(s0.shape, F32)],
        scratch_shapes=[pltpu.VMEM((nseq, RW_HEAD, w), F32)],
        compiler_params=_params("parallel", "arbitrary"),
        name="rwkv_scan",
    )(r2, y2, mx, z, s0, bonus, g, vecs, bd)


def rwkv_branch(c_rw, shift0, s0, lp, batch, seq):
    ncol = c_rw.shape[1]
    w = lp['rw_w0'].shape[0]
    nh = w // RW_HEAD
    r2, y2, bonus, g, mx, z = rwkv_chunk(c_rw, shift0, lp, batch, seq)
    s0t = jnp.transpose(s0, (0, 2, 1, 3)).reshape(batch, RW_HEAD, w)
    ovecs = jnp.stack([lp['rw_lnx_w'], lp['rw_lnx_b']])
    out, s_fin = rwkv_scan(r2, y2, mx, z, s0t, bonus, g, ovecs, batch, seq)
    s_fin = jnp.transpose(s_fin.reshape(batch, RW_HEAD, nh, RW_HEAD), (0, 2, 1, 3))
    return out, s_fin, c_rw.reshape(batch, seq, ncol)[:, -1].astype(F32)


def _att_prompt_body(q_ref, kp_ref, kc_ref, vp_ref, vc_ref, o_ref, l_ref, *, sub, dil):
    n = pl.program_id(1)
    scale = ATT_HD ** -0.5
    qi = lax.broadcasted_iota(jnp.int32, (sub, 2 * sub), 0)
    ki = lax.broadcasted_iota(jnp.int32, (sub, 2 * sub), 1) - sub
    dist = qi - ki
    valid = (dist >= 0) & (dist <= sub) & (n * sub + ki >= 0)
    lane = lax.broadcasted_iota(jnp.int32, (1, LANES), 1)
    hm = [(lane < ATT_HD).astype(F32), (lane >= ATT_HD).astype(F32)]
    for r in range(dil):
        rs = pl.ds(r, sub, stride=dil) if dil > 1 else slice(None)
        for p in range(q_ref.shape[1] // LANES):
            sl = slice(p * LANES, (p + 1) * LANES)
            q = q_ref[rs, sl] * scale
            kcat = jnp.concatenate([kp_ref[rs, sl], kc_ref[rs, sl]], axis=0).astype(BF16)
            vcat = jnp.concatenate([vp_ref[rs, sl], vc_ref[rs, sl]], axis=0).astype(BF16)
            o = jnp.zeros((sub, LANES), F32)
            lse = jnp.zeros((sub, LANES), F32)
            for m in hm:
                s = _dg((q * m).astype(BF16), kcat, NT)
                s = jnp.where(valid, s, -jnp.inf)
                mx = jnp.max(s, axis=-1, keepdims=True)
                pr = jnp.exp(s - mx)
                den = jnp.sum(pr, axis=-1, keepdims=True)
                o = o + _dg(pr.astype(BF16), vcat, NN) * (m / den)
                lse = lse + (mx + jnp.log(den)) * m
            o_ref[rs, sl] = o
            l_ref[rs, sl] = lse


def att_prompt_group(c_att, gi, batch, seq):
    win, dil = ATT_GROUPS[gi]
    sub = win // dil
    nblk = seq // win
    assert seq % win == 0
    cw = ATT_WIDTH if dil == 1 else LANES
    ppw = ATT_WIDTH // cw
    cur = lambda j: pl.BlockSpec((win, cw), lambda b, n, p: (b * nblk + n, (3 * gi + j) * ppw + p))
    prv = lambda j: pl.BlockSpec((win, cw),
                                 lambda b, n, p: (b * nblk + jnp.maximum(n - 1, 0), (3 * gi + j) * ppw + p))
    outb = pl.BlockSpec((win, cw), lambda b, n, p: (b * nblk + n, p))
    return pl.pallas_call(
        functools.partial(_att_prompt_body, sub=sub, dil=dil),
        grid=(batch, nblk, ppw),
        in_specs=[cur(0), prv(1), cur(1), prv(2), cur(2)],
        out_specs=[outb, outb],
        out_shape=[jax.ShapeDtypeStruct((batch * seq, ATT_WIDTH), F32)] * 2,
        compiler_params=_params("parallel", "parallel", "parallel"),
        name="att_prompt",
    )(c_att, c_att, c_att, c_att, c_att)


def _att_sample_body(q_ref, kn_ref, vn_ref, c_ref, bias_ref, o_ref, l_ref, *, dil, seq):
    hpq = MXU_DIM // ATT_HD
    trow = lax.broadcasted_iota(jnp.int32, (hpq * seq, seq), 0) % seq
    tcol = lax.broadcasted_iota(jnp.int32, (hpq * seq, seq), 1)
    new_ok = (tcol <= trow) & ((trow - tcol) % dil == 0)
    lane_head = lax.broadcasted_iota(jnp.int32, (seq, MXU_DIM), 1) // ATT_HD
    hm = [(lane_head == h).astype(F32) for h in range(hpq)]
    bias = bias_ref[...]
    for b in range(c_ref.shape[0]):
        rs = slice(b * seq, (b + 1) * seq)
        q = q_ref[rs, :] * (ATT_HD ** -0.5)
        kn, vn = kn_ref[rs, :], vn_ref[rs, :]
        for g in range(ATT_WIDTH // MXU_DIM):
            cs = slice(g * MXU_DIM, (g + 1) * MXU_DIM)
            qst = jnp.concatenate([q[:, cs] * m for m in hm], axis=0)
            sc = _bdot(qst, c_ref[b, 0, cs, :], NN) + bias
            sn = jnp.where(new_ok, _bdot(qst, kn[:, cs], NT), -jnp.inf)
            mx = jnp.maximum(jnp.max(sc, axis=-1, keepdims=True), jnp.max(sn, axis=-1, keepdims=True))
            pc = jnp.exp(sc - mx)
            pn = jnp.exp(sn - mx)
            den = jnp.sum(pc, axis=-1, keepdims=True) + jnp.sum(pn, axis=-1, keepdims=True)
            o = (_bdot(pc, c_ref[b, 1, cs, :], NT) + _bdot(pn, vn[:, cs], NN)) / den
            lse = mx + jnp.log(den)
            o_ref[rs, cs] = sum(o[h * seq:(h + 1) * seq] * hm[h] for h in range(hpq))
            l_ref[rs, cs] = sum(lse[h * seq:(h + 1) * seq] * hm[h] for h in range(hpq))


def att_sample_group(c_att, cache, layer, gi, batch, seq):
    win, dil = ATT_GROUPS[gi]
    depth, _, length = cache.shape[:3]
    assert length == win and seq <= win // dil
    cv = jnp.transpose(cache, (0, 1, 3, 4, 5, 2)).reshape(depth, batch, 2, ATT_WIDTH, length)
    hpq = MXU_DIM // ATT_HD
    t = np.tile(np.arange(seq), hpq)[:, None]
    j = np.arange(length)[None, :]
    bias = jnp.asarray(np.where((j >= t) & ((j - t) % dil == 0), 0.0, -np.inf), F32)
    nb = max(1, min(8, ATT_GROUPS[-1][0] // length))
    assert batch % nb == 0
    new = lambda c: pl.BlockSpec((nb * seq, ATT_WIDTH), lambda b: (b, 3 * gi + c))
    outb = pl.BlockSpec((nb * seq, ATT_WIDTH), lambda b: (b, 0))
    return pl.pallas_call(
        functools.partial(_att_sample_body, dil=dil, seq=seq),
        grid=(batch // nb,),
        in_specs=[new(0), new(1), new(2),
                  pl.BlockSpec((None, nb, 2, ATT_WIDTH, length), lambda b: (layer, b, 0, 0, 0)),
                  pl.BlockSpec(bias.shape, lambda b: (0, 0))],
        out_specs=[outb, outb],
        out_shape=[jax.ShapeDtypeStruct((batch * seq, ATT_WIDTH), F32)] * 2,
        compiler_params=_params("parallel"),
        name="att_sample",
    )(c_att, c_att, c_att, cv, bias)


def _merge_body(x_ref, gate_ref, hg_ref, rw_ref, o1_ref, o2_ref, o3_ref, l1_ref, l2_ref, l3_ref,
                whg_ref, wrw_ref, watt_ref, wout_ref, gn_ref, xo_ref, ho_ref):
    d = x_ref.shape[1]
    l1, l2, l3 = l1_ref[...], l2_ref[...], l3_ref[...]
    mx = jnp.maximum(jnp.maximum(l1, l2), l3)
    e1, e2, e3 = jnp.exp(l1 - mx), jnp.exp(l2 - mx), jnp.exp(l3 - mx)
    att = (e1 * o1_ref[...] + e2 * o2_ref[...] + e3 * o3_ref[...]) / (e1 + e2 + e3)
    gates = _sigmoid(gate_ref[...].astype(F32))
    bm = lambda a, w_ref: jnp.dot(a.astype(BF16), w_ref[...], preferred_element_type=F32)
    merged = (gates[:, 0:d] * bm(hg_ref[...], whg_ref)
              + gates[:, d:2 * d] * bm(rw_ref[...], wrw_ref)
              + gates[:, 2 * d:3 * d] * bm(att, watt_ref))
    xn = x_ref[...] + bm(merged, wout_ref)
    xo_ref[...] = xn
    ho_ref[...] = _rms(xn, gn_ref[...]).astype(ho_ref.dtype)


def merge_branches(x, c_gate, o_hg, o_rw, att_o, att_l, w_hg, w_rw, w_att, w_out, g_next):
    rows, d = x.shape
    tm = _row_tile(rows, 256)
    row = lambda a: pl.BlockSpec((tm, a.shape[1]), lambda i: (i, 0))
    full = lambda a: pl.BlockSpec(a.shape, lambda i: (0, 0))
    g2 = g_next.reshape(1, d)
    args = [x, c_gate, o_hg, o_rw, *att_o, *att_l]
    wts = [w_hg, w_rw, w_att, w_out, g2]
    return pl.pallas_call(
        _merge_body,
        grid=(rows // tm,),
        in_specs=[row(a) for a in args] + [full(a) for a in wts],
        out_specs=[pl.BlockSpec((tm, d), lambda i: (i, 0))] * 2,
        out_shape=[jax.ShapeDtypeStruct((rows, d), F32), jax.ShapeDtypeStruct((rows, d), BF16)],
        compiler_params=_params("parallel"),
        name="merge_branches",
    )(*args, *wts)


def _trunk_layer(x, h, lw, layer, batch, seq, state, last):
    d = x.shape[1]
    x, h = ffn_half_step(x, h, lw['ffn1_w_gu'], lw['ffn1_w_down'], lw['norm_mix'], BF16)
    c_hg = project(h, lw['w_in_hg'], 1024)
    c_rw = project(h, lw['w_in_rw'], lw['w_in_rw'].shape[1] // 2, BF16)
    c_att = project(h, lw['w_in_att'], 768)
    c_gate = project(h, lw['w_in_gate'], 1024, BF16)
    if state is None:
        o_hg, s_hg = hgrn_prompt(c_hg, lw['hg_params'], batch, seq)
        shift0 = jnp.zeros((batch, c_rw.shape[1]), F32)
        s_rw0 = jnp.zeros((batch, d // RW_HEAD, RW_HEAD, RW_HEAD), F32)
    else:
        s_hgrn, s_rwkv, s_shift, caches = state
        o_hg, s_hg = hgrn_sample(c_hg, lw['hg_params'], s_hgrn, layer, batch, seq)
        shift0, s_rw0 = s_shift[layer], s_rwkv[layer]
    o_rw, s_rw, shift = rwkv_branch(c_rw, shift0, s_rw0, lw, batch, seq)
    att_o, att_l, kv_rows = [], [], []
    c_att3 = c_att.reshape(batch, seq, -1)
    for gi, (win, dil) in enumerate(ATT_GROUPS):
        if state is None:
            o, l = att_prompt_group(c_att, gi, batch, seq)
        else:
            o, l = att_sample_group(c_att, caches[gi], layer, gi, batch, seq)
        att_o.append(o)
        att_l.append(l)
        keep = min(win, seq)
        kv = c_att3[:, seq - keep:, (3 * gi + 1) * ATT_WIDTH:(3 * gi + 3) * ATT_WIDTH]
        kv_rows.append(kv.reshape(batch, keep, 2, ATT_HEADS, ATT_HD))
    x, h = merge_branches(x, c_gate, o_hg, o_rw, att_o, att_l, lw['w_branch_hg'], lw['w_branch_rw'],
                          lw['w_branch_att'], lw['w_out'], lw['norm_ffn2'])
    x, h = ffn_half_step(x, h, lw['ffn2_w_gu'], lw['ffn2_w_down'], lw['norm_next'], F32 if last else BF16)
    return x, h, (s_hg, s_rw, shift, *kv_rows)


def _lower_bounds(hg_lb):
    p = jax.nn.softmax(hg_lb.astype(F32), axis=0)
    c = jnp.cumsum(p, axis=0)
    return c - c[0:1]


def kernel(x_prompt, x_sample, state_hgrn, state_rwkv, state_rwkv_shift, cache_att1_kv, cache_att2_kv, cache_att3_kv, norm_ffn1, ffn1_w_gu, ffn1_w_down, norm_mix, w_in, hg_lb, hg_gnorm, rw_mu, rw_w0, rw_w_up, rw_a0, rw_a_up, rw_g_up, rw_k_k, rw_k_a, rw_r_k, rw_lnx_w, rw_lnx_b, w_branch_hg, w_branch_rw, w_branch_att, w_out, norm_ffn2, ffn2_w_gu, ffn2_w_down, norm_final):
    depth = norm_ffn1.shape[0]
    bp, sp, d = x_prompt.shape
    bs, ss, _ = x_sample.shape
    hg_cols = 2 * HG_HEADS * HG_DK + 2 * hg_gnorm.shape[1]
    rw_cols = rw_mu.shape[1]
    att_cols = 3 * len(ATT_GROUPS) * ATT_WIDTH
    bounds = np.cumsum([0, hg_cols, rw_cols, att_cols, 3 * d])
    assert bounds[-1] == w_in.shape[2]
    lbs = _lower_bounds(hg_lb)
    layers = []
    for l in range(depth):
        lb = lbs[l]
        wl = w_in[l].astype(BF16)
        layers.append({
            'ffn1_w_gu': ffn1_w_gu[l].astype(BF16), 'ffn1_w_down': ffn1_w_down[l].astype(BF16),
            'ffn2_w_gu': ffn2_w_gu[l].astype(BF16), 'ffn2_w_down': ffn2_w_down[l].astype(BF16),
            'norm_mix': norm_mix[l], 'norm_ffn2': norm_ffn2[l],
            'norm_next': norm_ffn1[l + 1] if l + 1 < depth else norm_final,
            'w_in_hg': wl[:, bounds[0]:bounds[1]], 'w_in_rw': wl[:, bounds[1]:bounds[2]],
            'w_in_att': wl[:, bounds[2]:bounds[3]], 'w_in_gate': wl[:, bounds[3]:bounds[4]],
            'hg_params': jnp.stack([jnp.log(lb), jnp.log1p(-lb), 1.0 - lb, hg_gnorm[l]]),
            'rw_mu': rw_mu[l], 'rw_w0': rw_w0[l], 'rw_w_up': rw_w_up[l], 'rw_a0': rw_a0[l], 'rw_a_up': rw_a_up[l],
            'rw_g_up': rw_g_up[l], 'rw_k_k': rw_k_k[l], 'rw_k_a': rw_k_a[l], 'rw_r_k': rw_r_k[l],
            'rw_lnx_w': rw_lnx_w[l], 'rw_lnx_b': rw_lnx_b[l],
            'w_branch_hg': w_branch_hg[l].astype(BF16), 'w_branch_rw': w_branch_rw[l].astype(BF16),
            'w_branch_att': w_branch_att[l].astype(BF16), 'w_out': w_out[l].astype(BF16),
        })
    caches = (cache_att1_kv, cache_att2_kv, cache_att3_kv)
    sample_state = (state_hgrn, state_rwkv, state_rwkv_shift, caches)
    results = []
    for x3, batch, seq, state in ((x_prompt, bp, sp, None), (x_sample, bs, ss, sample_state)):
        x = x3.reshape(batch * seq, d)
        h = rmsnorm_rows(x, norm_ffn1[0], BF16)
        states = []
        for l in range(depth):
            x, h, st = _trunk_layer(x, h, layers[l], l, batch, seq, state, l + 1 == depth)
            states.append(st)
        stacked = [jnp.stack(z, axis=0) for z in zip(*states)]
        results.append((h.reshape(batch, seq, d), stacked))
    (yp, pst), (ys, sst) = results
    return (yp, ys, *pst, *sst)
```

```python
import functools
import math

import numpy as np
import jax
import jax.numpy as jnp
from jax import lax
from jax.experimental import pallas as pl
from jax.experimental.pallas import tpu as pltpu

F32 = jnp.float32
BF16 = jnp.bfloat16

HG_HEADS = 8
HG_DK = 128
RW_HEAD = 64
RW_LORA_W = 64
RW_LORA_A = 64
RW_LORA_G = 128
RW_LNX_EPS = 64e-5
ATT_GROUPS = ((128, 1), (512, 4), (2048, 16))
ATT_HEADS = 8
ATT_HD = 64
ATT_WIDTH = ATT_HEADS * ATT_HD
RMS_EPS = 1e-6
LANES = 128
SUBLANES = 8
MXU_DIM = 256
CHUNK = 64
HG_CHUNK = 128
VMEM_LIMIT = 48 * 1024 * 1024

NN = (((1,), (0,)), ((), ()))
NT = (((1,), (1,)), ((), ()))
TN = (((0,), (0,)), ((), ()))


def _dg(a, b, dims=NN):
    return lax.dot_general(a, b, dims, preferred_element_type=F32)


def _bdot(a, b, dims=NN):
    return lax.dot_general(a.astype(BF16), b.astype(BF16), dims, preferred_element_type=F32)


def _split2(x):
    hi = x.astype(BF16)
    lo = (x - hi.astype(F32)).astype(BF16)
    return hi, lo


def _split3(x):
    hi = x.astype(BF16)
    r = x - hi.astype(F32)
    mid = r.astype(BF16)
    lo = (r - mid.astype(F32)).astype(BF16)
    return hi, mid, lo


def _dot_sel(w, x, dims=NN):
    hi, mid, lo = _split3(x)
    return _dg(w, hi, dims) + _dg(w, mid, dims) + _dg(w, lo, dims)


def _dot_sel3(w3, x):
    return _dg(w3, jnp.concatenate(_split3(x), axis=0), NN)


def _dot_sel2(w2, x):
    return _dg(w2, jnp.concatenate(_split2(x), axis=0), NN)


def _dot_sel_rhs(x, w, dims=NN):
    hi, mid, lo = _split3(x)
    return _dg(hi, w, dims) + _dg(mid, w, dims) + _dg(lo, w, dims)


def _rms(x, g):
    return x * lax.rsqrt(jnp.mean(x * x, axis=-1, keepdims=True) + RMS_EPS) * g


def _sigmoid(x):
    return 1.0 / (1.0 + jnp.exp(-x))


def _params(*sem):
    return pltpu.CompilerParams(dimension_semantics=sem, vmem_limit_bytes=VMEM_LIMIT)


def _row_tile(rows, want):
    t = min(rows, want)
    assert rows % t == 0
    return t


def _rmsnorm_body(x_ref, g_ref, o_ref):
    o_ref[...] = _rms(x_ref[...], g_ref[...]).astype(o_ref.dtype)


def rmsnorm_rows(x, g, out_dtype):
    rows, d = x.shape
    tm = _row_tile(rows, 1024)
    return pl.pallas_call(
        _rmsnorm_body,
        grid=(rows // tm,),
        in_specs=[pl.BlockSpec((tm, d), lambda i: (i, 0)), pl.BlockSpec((1, d), lambda i: (0, 0))],
        out_specs=pl.BlockSpec((tm, d), lambda i: (i, 0)),
        out_shape=jax.ShapeDtypeStruct((rows, d), out_dtype),
        compiler_params=_params("parallel"),
        name="rmsnorm",
    )(x, g.reshape(1, d))


def _ffn_body(x_ref, h_ref, wg_ref, wu_ref, wd_ref, gn_ref, xo_ref, ho_ref, acc_ref):
    j = pl.program_id(1)

    @pl.when(j == 0)
    def _():
        acc_ref[...] = jnp.zeros_like(acc_ref)

    h = h_ref[...]
    gate = jnp.dot(h, wg_ref[...], preferred_element_type=F32)
    up = jnp.dot(h, wu_ref[...], preferred_element_type=F32)
    act = (gate * _sigmoid(gate) * up).astype(BF16)
    acc_ref[...] += jnp.dot(act, wd_ref[...], preferred_element_type=F32)

    @pl.when(j == pl.num_programs(1) - 1)
    def _():
        xn = x_ref[...] + 0.5 * acc_ref[...]
        xo_ref[...] = xn
        ho_ref[...] = _rms(xn, gn_ref[...]).astype(ho_ref.dtype)


def ffn_half_step(x, h, w_gu, w_down, g_next, next_dtype):
    rows, d = x.shape
    dff = w_down.shape[0]
    tm = _row_tile(rows, 1024)
    tf = 256
    nf = dff // tf
    assert dff % tf == 0
    return pl.pallas_call(
        _ffn_body,
        grid=(rows // tm, nf),
        in_specs=[
            pl.BlockSpec((tm, d), lambda i, j: (i, 0)),
            pl.BlockSpec((tm, d), lambda i, j: (i, 0)),
            pl.BlockSpec((d, tf), lambda i, j: (0, j)),
            pl.BlockSpec((d, tf), lambda i, j: (0, j + nf)),
            pl.BlockSpec((tf, d), lambda i, j: (j, 0)),
            pl.BlockSpec((1, d), lambda i, j: (0, 0)),
        ],
        out_specs=[pl.BlockSpec((tm, d), lambda i, j: (i, 0)), pl.BlockSpec((tm, d), lambda i, j: (i, 0))],
        out_shape=[jax.ShapeDtypeStruct((rows, d), F32), jax.ShapeDtypeStruct((rows, d), next_dtype)],
        scratch_shapes=[pltpu.VMEM((tm, d), F32)],
        compiler_params=_params("parallel", "arbitrary"),
        name="ffn_half_step",
    )(x, h, w_gu, w_gu, w_down, g_next.reshape(1, d))


def _proj_body(h_ref, w_ref, o_ref):
    o_ref[...] = jnp.dot(h_ref[...], w_ref[...], preferred_element_type=F32).astype(o_ref.dtype)


def project(h, w, tn, out_dtype=F32):
    rows, d = h.shape
    n = w.shape[1]
    tm = _row_tile(rows, 2048)
    assert n % tn == 0
    return pl.pallas_call(
        _proj_body,
        grid=(rows // tm, n // tn),
        in_specs=[pl.BlockSpec((tm, d), lambda i, j: (i, 0)), pl.BlockSpec((d, tn), lambda i, j: (0, j))],
        out_specs=pl.BlockSpec((tm, tn), lambda i, j: (i, j)),
        out_shape=jax.ShapeDtypeStruct((rows, n), out_dtype),
        compiler_params=_params("parallel", "parallel"),
        name="project",
    )(h, w)


def _level_constants(rows, seq):
    levels = int(math.log2(seq))
    assert 2 ** levels == seq and rows % seq == 0
    t = np.arange(rows)
    r = t[None, :]
    ws, ms = [], [np.eye(rows, dtype=bool)]
    for l in range(1, levels + 1):
        m = 2 ** l
        hm = m // 2
        par, pos = t // m, t % m
        sec = pos >= hm
        ref = par * m + hm - 1
        w = np.where(sec[:, None], (r > ref[:, None]) & (r <= t[:, None]), (r > t[:, None]) & (r <= ref[:, None]))
        ws.append(w)
        ms.append((par[:, None] == par[None, :]) & sec[:, None] & (~sec)[None, :])
    same = (t // seq)[:, None] == (t // seq)[None, :]
    ws.append(same & (r <= t[:, None]))
    ws.append(same & (r > t[:, None]))
    wmat = np.concatenate(ws, 0).astype(np.float32)
    return np.concatenate([wmat] * 2, axis=1), np.stack(ms).astype(np.float32)


def _hgrn_gates(f, loglb, log1mlb, onemlb):
    ls = jnp.minimum(f, 0.0) - jnp.log(1.0 + jnp.exp(-jnp.abs(f)))
    a = log1mlb + ls
    log_f = jnp.maximum(a, loglb) + jnp.log(1.0 + jnp.exp(-jnp.abs(a - loglb)))
    key = onemlb * jnp.exp(ls - f)
    return -log_f, key


def _hgrn_intra(q, k, e_lv, masks, nlev):
    a = _dg(q, k, NT) * masks[0]
    for l in range(1, nlev + 1):
        e = e_lv[l - 1]
        a = a + _dg(q * e, k * e, NT) * masks[l]
    return a


def _hgrn_prompt_body(q_ref, f_ref, i_ref, g_ref, lb_ref, w_ref, m_ref, o_ref, s_ref, st_ref, *, nlev):
    c = pl.program_id(1)
    rows = q_ref.shape[0]

    @pl.when(c == 0)
    def _():
        st_ref[...] = jnp.zeros_like(st_ref)

    nlf, key = _hgrn_gates(f_ref[...], lb_ref[0:1, :], lb_ref[1:2, :], lb_ref[2:3, :])
    dec = jnp.exp(-_dot_sel2(w_ref[...], nlf))
    qraw = q_ref[...]
    qact = qraw * _sigmoid(qraw)
    val = i_ref[...]
    og = g_ref[...]
    masks = [m_ref[l] for l in range(nlev + 1)]
    for h in range(HG_HEADS):
        sl = slice(h * HG_DK, (h + 1) * HG_DK)
        q, k, v = qact[:, sl], key[:, sl], val[:, sl]
        e_lv = [dec[l * rows:(l + 1) * rows, sl] for l in range(nlev)]
        e_cum = dec[nlev * rows:(nlev + 1) * rows, sl]
        e_suf = dec[(nlev + 1) * rows:(nlev + 2) * rows, sl]
        a = _hgrn_intra(q, k, e_lv, masks, nlev)
        st = st_ref[h]
        o = _dg(q * e_cum, st, NT) + _dg(a, v, NN)
        st_new = st * e_cum[rows - 1:rows, :] + _dg(v, k * e_suf, TN)
        st_ref[h] = st_new
        y = _rms(o, lb_ref[3:4, sl])
        ogh = og[:, sl]
        o_ref[:, sl] = (y * (ogh * _sigmoid(ogh))).astype(o_ref.dtype)

    @pl.when(c == pl.num_programs(1) - 1)
    def _():
        for h in range(HG_HEADS):
            s_ref[0, h] = st_ref[h].T


def hgrn_prompt(c_hg, lbp, batch, seq):
    width = c_hg.shape[1] // 4
    rows = min(seq, HG_CHUNK)
    nlev = int(math.log2(rows))
    wmat, masks = _level_constants(rows, rows)
    nc = seq // rows
    col = lambda j: pl.BlockSpec((rows, width), lambda b, c: (b * nc + c, j))
    return pl.pallas_call(
        functools.partial(_hgrn_prompt_body, nlev=nlev),
        grid=(batch, nc),
        in_specs=[col(0), col(1), col(2), col(3),
                  pl.BlockSpec(lbp.shape, lambda b, c: (0, 0)),
                  pl.BlockSpec(wmat.shape, lambda b, c: (0, 0)),
                  pl.BlockSpec(masks.shape, lambda b, c: (0, 0, 0))],
        out_specs=[pl.BlockSpec((rows, width), lambda b, c: (b * nc + c, 0)),
                   pl.BlockSpec((1, HG_HEADS, HG_DK, HG_DK), lambda b, c: (b, 0, 0, 0))],
        out_shape=[jax.ShapeDtypeStruct((batch * seq, width), BF16),
                   jax.ShapeDtypeStruct((batch, HG_HEADS, HG_DK, HG_DK), F32)],
        scratch_shapes=[pltpu.VMEM((HG_HEADS, HG_DK, HG_DK), F32)],
        compiler_params=_params("parallel", "arbitrary"),
        name="hgrn_prompt",
    )(c_hg, c_hg, c_hg, c_hg, lbp, jnp.asarray(wmat, BF16), jnp.asarray(masks))


def _hgrn_sample_body(q_ref, f_ref, i_ref, g_ref, lb_ref, w_ref, m_ref, s0_ref, o_ref, s_ref, *, nlev, seq):
    rows = q_ref.shape[0]
    nseq = rows // seq
    nlf, key = _hgrn_gates(f_ref[...], lb_ref[0:1, :], lb_ref[1:2, :], lb_ref[2:3, :])
    dec = jnp.exp(-_dot_sel2(w_ref[...], nlf))
    qraw = q_ref[...]
    q = qraw * _sigmoid(qraw)
    v = i_ref[...]
    og = g_ref[...]
    masks = [m_ref[l] for l in range(nlev + 1)]
    e_lv = [dec[l * rows:(l + 1) * rows] for l in range(nlev)]
    e_cum = dec[nlev * rows:(nlev + 1) * rows]
    e_suf = dec[(nlev + 1) * rows:(nlev + 2) * rows]
    a = _hgrn_intra(q, key, e_lv, masks, nlev)
    o_intra = _dg(a, v, NN)
    qc = q * e_cum
    ks = key * e_suf
    ones = jnp.ones((seq, HG_DK), BF16)
    outs = []
    for b in range(nseq):
        rs = slice(b * seq, (b + 1) * seq)
        s0 = s0_ref[b, 0]
        outs.append(_dg(qc[rs], s0, NN))
        total = _dot_sel_rhs(nlf[rs], ones, TN)
        s_ref[b, 0] = jnp.exp(-total) * s0 + _dg(ks[rs], v[rs], TN)
    o = o_intra + jnp.concatenate(outs, axis=0)
    y = _rms(o, lb_ref[3:4, :])
    o_ref[...] = (y * (og * _sigmoid(og))).astype(o_ref.dtype)


def hgrn_sample(c_hg, lbp, state, layer, batch, seq):
    width = c_hg.shape[1] // 4
    nh = width // HG_DK
    rows = CHUNK
    nseq = rows // seq
    nlev = int(math.log2(seq))
    wmat, masks = _level_constants(rows, seq)
    col = lambda j: pl.BlockSpec((rows, HG_DK), lambda i, h: (i, j * nh + h))
    lbspec = pl.BlockSpec((lbp.shape[0], HG_DK), lambda i, h: (0, h))
    return pl.pallas_call(
        functools.partial(_hgrn_sample_body, nlev=nlev, seq=seq),
        grid=(batch // nseq, nh),
        in_specs=[col(0), col(1), col(2), col(3), lbspec,
                  pl.BlockSpec(wmat.shape, lambda i, h: (0, 0)),
                  pl.BlockSpec(masks.shape, lambda i, h: (0, 0, 0)),
                  pl.BlockSpec((None, nseq, 1, HG_DK, HG_DK), lambda i, h: (layer, i, h, 0, 0))],
        out_specs=[pl.BlockSpec((rows, HG_DK), lambda i, h: (i, h)),
                   pl.BlockSpec((nseq, 1, HG_DK, HG_DK), lambda i, h: (i, h, 0, 0))],
        out_shape=[jax.ShapeDtypeStruct((batch * seq, width), BF16),
                   jax.ShapeDtypeStruct((batch, nh, HG_DK, HG_DK), F32)],
        compiler_params=_params("parallel", "parallel"),
        name="hgrn_sample",
    )(c_hg, c_hg, c_hg, c_hg, lbp, jnp.asarray(wmat, BF16), jnp.asarray(masks), state)


def _group_sum(x, bd):
    return _dg(jnp.concatenate(_split2(x), axis=1), bd, NN)


def _group_sum_wide(x, bd):
    return jnp.concatenate([_group_sum(x[:, j * LANES:(j + 1) * LANES], bd) for j in range(x.shape[1] // LANES)],
                           axis=1)


def _head_blockdiag(n, head):
    i = np.arange(n) // head
    bd = (i[:, None] == i[None, :]).astype(np.float32)
    return np.concatenate([bd, bd], axis=0)


def _rwkv_consts(rows, seq):
    t = np.arange(rows)
    sq = t // seq
    same = sq[:, None] == sq[None, :]
    r = t[None, :]
    pre = same & (r <= t[:, None])
    mid = sq * seq + seq // 2 - 1
    premid = same & (r <= mid[:, None])
    suf = same & (r > t[:, None])
    wmat = np.concatenate([pre.astype(np.float32) - premid.astype(np.float32), pre, suf], 0)
    t2 = np.concatenate([t, t])
    h2 = np.concatenate([0 * t, 0 * t + 1])
    s2 = t2 // seq
    blk = (h2[:, None] == h2[None, :]) & (s2[:, None] == s2[None, :])
    strict = blk & (t2[None, :] < t2[:, None])
    incl = blk & (t2[None, :] <= t2[:, None])
    lv = []
    for l in range(1, int(math.log2(seq)) + 1):
        m = 2 ** l
        par, pos = t2 // m, t2 % m
        lv.append(blk & (par[:, None] == par[None, :]) & (pos[:, None] >= m // 2) & (pos[None, :] < m // 2))
    masks = np.stack([strict, incl, np.eye(2 * rows, dtype=bool)] + lv).astype(np.float32)
    first = (t[:, None] == (np.arange(rows // seq) * seq)[None, :]).astype(np.float32)
    return np.concatenate([wmat] * 3, axis=1), masks, first


def _stack2(x, m0, m1):
    return jnp.concatenate([x * m0, x * m1], axis=0)


def _rwkv_chunk_body(c_ref, cp_ref, sh_ref, mu_ref, vec_ref, wup_ref, aup_ref, gup_ref, bd_ref, w_ref, m_ref,
                     f_ref, r2_ref, y2_ref, bn_ref, g_ref, mx_ref, z_ref, *, seq, nchunk):
    rows = c_ref.shape[0]
    w = r2_ref.shape[1]
    npair = w // LANES
    nseq = rows // seq
    nlev = int(math.log2(seq))

    c = c_ref[...].astype(F32)
    rolled = pltpu.roll(c, 1, 0)
    rowi = lax.broadcasted_iota(jnp.int32, (rows, 1), 0)
    if nseq == 1:
        last = cp_ref.shape[0] - 1
        head_row = jnp.where(pl.program_id(0) % nchunk == 0, sh_ref[...], cp_ref[last:last + 1, :].astype(F32))
        prev = jnp.where(rowi == 0, head_row, rolled)
    else:
        prev = jnp.where(rowi % seq == 0, _dot_sel(f_ref[...], sh_ref[...]), rolled)
    xs = c + mu_ref[...] * (prev - c)

    rr, k, vv = xs[:, 0:w], xs[:, w:2 * w], xs[:, 2 * w:3 * w]
    lora = xs[:, 3 * w:3 * w + RW_LORA_W + RW_LORA_A]
    gl = xs[:, 3 * w + RW_LORA_W + RW_LORA_A:]
    w0, a0, k_k, k_a, r_k = (vec_ref[i:i + 1, :] for i in range(5))
    lw = -math.exp(-0.5) * _sigmoid(w0 + _dg(jnp.tanh(lora), wup_ref[...]))
    a = _sigmoid(a0 + _dg(lora, aup_ref[...]))
    g_ref[...] = _dg(_sigmoid(gl), gup_ref[...]).astype(g_ref.dtype)
    bd = bd_ref[...]
    kkr = k * k_k
    kk = kkr * lax.rsqrt(jnp.maximum(_group_sum_wide(kkr * kkr, bd), 1e-24))
    kx = k * (1.0 + (a - 1.0) * k_a)
    bn_ref[...] = (_group_sum_wide(rr * kx * r_k, bd) * vv).astype(bn_ref.dtype)
    bb = kk * a
    aa = -kk

    d = _dot_sel3(w_ref[...], lw)
    dm, gin, suf = d[0:rows], d[rows:2 * rows], d[2 * rows:3 * rows]
    e_inv = jnp.exp(-dm)
    e_abs = jnp.exp(gin)
    e_suf = jnp.exp(suf)
    a_mid = aa * jnp.exp(dm - lw)
    r_mid = rr * jnp.exp(dm)
    b_inv = bb * e_inv
    k_inv = kx * e_inv
    a_abs = aa * jnp.exp(gin - lw)
    r_abs = rr * e_abs
    b_end = bb * e_suf
    k_end = kx * e_suf

    lane = lax.broadcasted_iota(jnp.int32, (rows, LANES), 1)
    m0 = (lane < RW_HEAD).astype(F32)
    m1 = 1.0 - m0
    masks = [m_ref[i] for i in range(3 + nlev)]
    n2 = 2 * rows
    pairs = range(npair)
    s2 = lambda x, p: _stack2(x[:, p * LANES:(p + 1) * LANES], m0, m1)

    big = [_bdot(jnp.concatenate([s2(a_mid, p), s2(r_mid, p)], axis=0),
                 jnp.concatenate([s2(b_inv, p), s2(k_inv, p)], axis=0), NT) for p in pairs]
    nmat = [big[p][0:n2, 0:n2] * masks[0] for p in pairs]
    tinv = [masks[2] + nmat[p] * masks[3] for p in pairs]
    for l in range(2, nlev + 1):
        tinv = [tinv[p] + _bdot(_bdot(tinv[p], nmat[p] * masks[2 + l]), tinv[p]) for p in pairs]
    v_st = [s2(vv, p) for p in pairs]
    cmask = jnp.concatenate([masks[0], masks[1]], axis=0)
    kv = [_bdot(big[p][:, n2:] * cmask, v_st[p]) for p in pairs]
    pq = [_bdot(tinv[p], jnp.concatenate([s2(a_abs, p), kv[p][0:n2]], axis=1)) for p in pairs]
    ry = [_bdot(big[p][n2:, 0:n2] * masks[1], pq[p]) for p in pairs]
    eye2 = masks[2][0:RW_HEAD, 0:LANES] + masks[2][RW_HEAD:LANES, 0:LANES]
    for p in pairs:
        sl = slice(p * LANES, (p + 1) * LANES)
        r2 = s2(r_abs, p) + ry[p][:, 0:LANES]
        y2 = ry[p][:, LANES:] + kv[p][n2:]
        r2_ref[:, sl] = (r2[0:rows] + r2[rows:]).astype(r2_ref.dtype)
        y2_ref[:, sl] = (y2[0:rows] + y2[rows:]).astype(y2_ref.dtype)
        b_st = s2(b_end, p)
        k_st = s2(k_end, p)
        for b in range(nseq):
            pick = lambda x: jnp.concatenate([x[b * seq:(b + 1) * seq], x[rows + b * seq:rows + (b + 1) * seq]], axis=0)
            mz = _bdot(pick(pq[p]), pick(b_st), TN)
            zz = mz[LANES:] + _bdot(pick(v_st[p]), pick(k_st), TN)
            g_end = e_abs[b * seq + seq - 1:b * seq + seq, sl]
            hs = slice(b * RW_HEAD, (b + 1) * RW_HEAD)
            mx_ref[hs, sl] = (mz[0:RW_HEAD] + mz[RW_HEAD:LANES] + eye2 * g_end).astype(mx_ref.dtype)
            z_ref[hs, sl] = (zz[0:RW_HEAD] + zz[RW_HEAD:]).astype(z_ref.dtype)


def rwkv_chunk(c_rw, shift0, lp, batch, seq):
    rows_all, ncol = c_rw.shape
    w = lp['rw_w0'].shape[0]
    rows = CHUNK
    cseq = min(seq, rows)
    nseq = rows // cseq
    nchunk = seq // cseq
    wmat, masks, first = _rwkv_consts(rows, cseq)
    vecs = jnp.stack([lp['rw_w0'], lp['rw_a0'], lp['rw_k_k'], lp['rw_k_a'], lp['rw_r_k'].reshape(w)])
    wup = jnp.concatenate([lp['rw_w_up'], jnp.zeros_like(lp['rw_a_up'])], axis=0)
    aup = jnp.concatenate([jnp.zeros_like(lp['rw_w_up']), lp['rw_a_up']], axis=0)
    bd = jnp.asarray(_head_blockdiag(LANES, RW_HEAD), BF16)
    mu2 = lp['rw_mu'].reshape(1, ncol)
    if nseq == 1:
        sh = shift0.reshape(batch, 1, ncol)
        sh_spec = pl.BlockSpec((None, 1, ncol), lambda i: (i // nchunk, 0, 0))
    else:
        sh = shift0
        sh_spec = pl.BlockSpec((nseq, ncol), lambda i: (i, 0))
    consts = [mu2, vecs, wup, aup, lp['rw_g_up'], bd, jnp.asarray(wmat, BF16), jnp.asarray(masks),
              jnp.asarray(first, BF16)]
    full = lambda a: pl.BlockSpec(a.shape, lambda i: (0,) * a.ndim)
    row = pl.BlockSpec((rows, w), lambda i: (i, 0))
    st = pl.BlockSpec((nseq * RW_HEAD, w), lambda i: (i, 0))
    nst = rows_all // cseq * RW_HEAD
    pblk = SUBLANES * (4 // c_rw.dtype.itemsize)
    per = rows // pblk
    return pl.pallas_call(
        functools.partial(_rwkv_chunk_body, seq=cseq, nchunk=nchunk),
        grid=(rows_all // rows,),
        in_specs=[pl.BlockSpec((rows, ncol), lambda i: (i, 0)),
                  pl.BlockSpec((pblk, ncol), lambda i: (jnp.maximum(i * per - 1, 0), 0)),
                  sh_spec] + [full(a) for a in consts],
        out_specs=[row, row, row, row, st, st],
        out_shape=[jax.ShapeDtypeStruct((rows_all, w), BF16)] * 4 + [jax.ShapeDtypeStruct((nst, w), BF16)] * 2,
        compiler_params=_params("parallel"),
        name="rwkv_chunk",
    )(c_rw, c_rw, sh, *consts)


def _rwkv_scan_body(r2_ref, y2_ref, mx_ref, z_ref, s0_ref, bn_ref, g_ref, vec_ref, bd_ref,
                    o_ref, s_ref, st_ref, *, seq):
    c = pl.program_id(1)
    rows = r2_ref.shape[0]
    nunit = rows // seq
    nstate = st_ref.shape[0]
    w = r2_ref.shape[1]

    @pl.when(c == 0)
    def _():
        st_ref[...] = s0_ref[...]

    lane = lax.broadcasted_iota(jnp.int32, (RW_HEAD, LANES), 1)
    m0 = (lane < RW_HEAD).astype(F32)
    m1 = 1.0 - m0
    ys = []
    for u in range(nunit):
        rs = slice(u * seq, (u + 1) * seq)
        hs = slice(u * RW_HEAD, (u + 1) * RW_HEAD)
        b = u if nstate > 1 else 0
        yb = []
        for p in range(w // LANES):
            sl = slice(p * LANES, (p + 1) * LANES)
            s = st_ref[b, :, sl]
            yb.append(_bdot(r2_ref[rs, sl], _stack2(s, m0, m1), NT) + y2_ref[rs, sl].astype(F32))
            st_ref[b, :, sl] = (_bdot(s, _stack2(mx_ref[hs, sl].astype(F32), m0, m1), NN)
                                + z_ref[hs, sl].astype(F32))
        ys.append(jnp.concatenate(yb, axis=1))
    y = jnp.concatenate(ys, axis=0) if nunit > 1 else ys[0]
    bd = bd_ref[...]
    mean = _group_sum_wide(y, bd) * (1.0 / RW_HEAD)
    yc = y - mean
    var = _group_sum_wide(yc * yc, bd) * (1.0 / RW_HEAD)
    yn = yc * lax.rsqrt(var + RW_LNX_EPS) * vec_ref[0:1, :] + vec_ref[1:2, :]
    o_ref[...] = ((yn + bn_ref[...]) * g_ref[...]).astype(o_ref.dtype)

    @pl.when(c == pl.num_programs(1) - 1)
    def _():
        s_ref[...] = st_ref[...]


def rwkv_scan(r2, y2, mx, z, s0, bonus, g, vecs, batch, seq):
    rows_all, w = r2.shape
    cseq = min(seq, CHUNK)
    if seq > CHUNK:
        rows, nseq = min(seq, 2 * CHUNK), 1
    else:
        rows, nseq = CHUNK, CHUNK // cseq
    nunit = rows // cseq
    nc = max(seq // rows, 1)
    bd = jnp.asarray(_head_blockdiag(LANES, RW_HEAD), BF16)
    row = pl.BlockSpec((rows, w), lambda i, c: (i * nc + c, 0))
    st = pl.BlockSpec((nunit * RW_HEAD, w), lambda i, c: (i * nc + c, 0))
    sblk = pl.BlockSpec((nseq, RW_HEAD, w), lambda i, c: (i, 0, 0))
    return pl.pallas_call(
        functools.partial(_rwkv_scan_body, seq=cseq),
        grid=(batch // nseq, nc),
        in_specs=[row, row, st, st, sblk, row, row,
                  pl.BlockSpec(vecs.shape, lambda i, c: (0, 0)), pl.BlockSpec(bd.shape, lambda i, c: (0, 0))],
        out_specs=[row, sblk],
        out_shape=[jax.ShapeDtypeStruct((rows_all, w), BF16), jax.ShapeDtypeStruct(s0.shape, F32)],
        scratch_shapes=[pltpu.VMEM((nseq, RW_HEAD, w), F32)],
        compiler_params=_params("parallel", "arbitrary"),
        name="rwkv_scan",
    )(r2, y2, mx, z, s0, bonus, g, vecs, bd)


def rwkv_branch(c_rw, shift0, s0, lp, batch, seq):
    ncol = c_rw.shape[1]
    w = lp['rw_w0'].shape[0]
    nh = w // RW_HEAD
    r2, y2, bonus, g, mx, z = rwkv_chunk(c_rw, shift0, lp, batch, seq)
    s0t = jnp.transpose(s0, (0, 2, 1, 3)).reshape(batch, RW_HEAD, w)
    ovecs = jnp.stack([lp['rw_lnx_w'], lp['rw_lnx_b']])
    out, s_fin = rwkv_scan(r2, y2, mx, z, s0t, bonus, g, ovecs, batch, seq)
    s_fin = jnp.transpose(s_fin.reshape(batch, RW_HEAD, nh, RW_HEAD), (0, 2, 1, 3))
    return out, s_fin, c_rw.reshape(batch, seq, ncol)[:, -1].astype(F32)


def _att_prompt_body(q_ref, kp_ref, kc_ref, vp_ref, vc_ref, o_ref, l_ref, *, sub, dil):
    n = pl.program_id(1)
    scale = ATT_HD ** -0.5
    qi = lax.broadcasted_iota(jnp.int32, (sub, 2 * sub), 0)
    ki = lax.broadcasted_iota(jnp.int32, (sub, 2 * sub), 1) - sub
    dist = qi - ki
    valid = (dist >= 0) & (dist <= sub) & (n * sub + ki >= 0)
    lane = lax.broadcasted_iota(jnp.int32, (1, LANES), 1)
    hm = [(lane < ATT_HD).astype(F32), (lane >= ATT_HD).astype(F32)]
    for r in range(dil):
        rs = pl.ds(r, sub, stride=dil) if dil > 1 else slice(None)
        for p in range(q_ref.shape[1] // LANES):
            sl = slice(p * LANES, (p + 1) * LANES)
            q = q_ref[rs, sl] * scale
            kcat = jnp.concatenate([kp_ref[rs, sl], kc_ref[rs, sl]], axis=0).astype(BF16)
            vcat = jnp.concatenate([vp_ref[rs, sl], vc_ref[rs, sl]], axis=0).astype(BF16)
            vone = jnp.concatenate([vcat, jnp.ones_like(vcat)], axis=1)
            o = jnp.zeros((sub, LANES), F32)
            lse = jnp.zeros((sub, LANES), F32)
            for m in hm:
                s = _dg((q * m).astype(BF16), kcat, NT)
                s = jnp.where(valid, s, -jnp.inf)
                mx = jnp.max(s, axis=-1, keepdims=True)
                pr = jnp.exp((s - mx).astype(BF16))
                od = _dg(pr, vone, NN)
                den = od[:, LANES:]
                o = o + od[:, 0:LANES] * (m / den)
                lse = lse + (mx + jnp.log(den)) * m
            o_ref[rs, sl] = o
            l_ref[rs, sl] = lse


def att_prompt_group(c_att, gi, batch, seq):
    win, dil = ATT_GROUPS[gi]
    sub = win // dil
    nblk = seq // win
    assert seq % win == 0
    cw = ATT_WIDTH if dil == 1 else LANES
    ppw = ATT_WIDTH // cw
    cur = lambda j: pl.BlockSpec((win, cw), lambda b, n, p: (b * nblk + n, (3 * gi + j) * ppw + p))
    prv = lambda j: pl.BlockSpec((win, cw),
                                 lambda b, n, p: (b * nblk + jnp.maximum(n - 1, 0), (3 * gi + j) * ppw + p))
    outb = pl.BlockSpec((win, cw), lambda b, n, p: (b * nblk + n, p))
    return pl.pallas_call(
        functools.partial(_att_prompt_body, sub=sub, dil=dil),
        grid=(batch, nblk, ppw),
        in_specs=[cur(0), prv(1), cur(1), prv(2), cur(2)],
        out_specs=[outb, outb],
        out_shape=[jax.ShapeDtypeStruct((batch * seq, ATT_WIDTH), F32)] * 2,
        compiler_params=_params("parallel", "parallel", "parallel"),
        name="att_prompt",
    )(c_att, c_att, c_att, c_att, c_att)


def _att_sample_body(q_ref, kn_ref, vn_ref, c_ref, bias_ref, o_ref, l_ref, *, dil, seq):
    hpq = MXU_DIM // ATT_HD
    trow = lax.broadcasted_iota(jnp.int32, (hpq * seq, seq), 0) % seq
    tcol = lax.broadcasted_iota(jnp.int32, (hpq * seq, seq), 1)
    new_ok = (tcol <= trow) & ((trow - tcol) % dil == 0)
    lane_head = lax.broadcasted_iota(jnp.int32, (seq, MXU_DIM), 1) // ATT_HD
    hm = [(lane_head == h).astype(F32) for h in range(hpq)]
    bias = bias_ref[...]
    for b in range(c_ref.shape[0]):
        rs = slice(b * seq, (b + 1) * seq)
        q = q_ref[rs, :] * (ATT_HD ** -0.5)
        kn, vn = kn_ref[rs, :], vn_ref[rs, :]
        for g in range(ATT_WIDTH // MXU_DIM):
            cs = slice(g * MXU_DIM, (g + 1) * MXU_DIM)
            qst = jnp.concatenate([q[:, cs] * m for m in hm], axis=0)
            sc = _bdot(qst, c_ref[b, 0, cs, :], NN) + bias
            sn = jnp.where(new_ok, _bdot(qst, kn[:, cs], NT), -jnp.inf)
            mx = jnp.maximum(jnp.max(sc, axis=-1, keepdims=True), jnp.max(sn, axis=-1, keepdims=True))
            pc = jnp.exp(sc - mx)
            pn = jnp.exp(sn - mx)
            den = jnp.sum(pc, axis=-1, keepdims=True) + jnp.sum(pn, axis=-1, keepdims=True)
            o = (_bdot(pc, c_ref[b, 1, cs, :], NT) + _bdot(pn, vn[:, cs], NN)) / den
            lse = mx + jnp.log(den)
            o_ref[rs, cs] = sum(o[h * seq:(h + 1) * seq] * hm[h] for h in range(hpq))
            l_ref[rs, cs] = sum(lse[h * seq:(h + 1) * seq] * hm[h] for h in range(hpq))


def att_sample_group(c_att, cache, layer, gi, batch, seq):
    win, dil = ATT_GROUPS[gi]
    depth, _, length = cache.shape[:3]
    assert length == win and seq <= win // dil
    cv = jnp.transpose(cache, (0, 1, 3, 4, 5, 2)).reshape(depth, batch, 2, ATT_WIDTH, length)
    hpq = MXU_DIM // ATT_HD
    t = np.tile(np.arange(seq), hpq)[:, None]
    j = np.arange(length)[None, :]
    bias = jnp.asarray(np.where((j >= t) & ((j - t) % dil == 0), 0.0, -np.inf), F32)
    nb = max(1, min(8, ATT_GROUPS[-1][0] // length))
    assert batch % nb == 0
    new = lambda c: pl.BlockSpec((nb * seq, ATT_WIDTH), lambda b: (b, 3 * gi + c))
    outb = pl.BlockSpec((nb * seq, ATT_WIDTH), lambda b: (b, 0))
    return pl.pallas_call(
        functools.partial(_att_sample_body, dil=dil, seq=seq),
        grid=(batch // nb,),
        in_specs=[new(0), new(1), new(2),
                  pl.BlockSpec((None, nb, 2, ATT_WIDTH, length), lambda b: (layer, b, 0, 0, 0)),
                  pl.BlockSpec(bias.shape, lambda b: (0, 0))],
        out_specs=[outb, outb],
        out_shape=[jax.ShapeDtypeStruct((batch * seq, ATT_WIDTH), F32)] * 2,
        compiler_params=_params("parallel"),
        name="att_sample",
    )(c_att, c_att, c_att, cv, bias)


def _merge_body(x_ref, gate_ref, hg_ref, rw_ref, o1_ref, o2_ref, o3_ref, l1_ref, l2_ref, l3_ref,
                whg_ref, wrw_ref, watt_ref, wout_ref, gn_ref, xo_ref, ho_ref):
    d = x_ref.shape[1]
    l1, l2, l3 = l1_ref[...], l2_ref[...], l3_ref[...]
    mx = jnp.maximum(jnp.maximum(l1, l2), l3)
    e1, e2, e3 = jnp.exp(l1 - mx), jnp.exp(l2 - mx), jnp.exp(l3 - mx)
    att = (e1 * o1_ref[...] + e2 * o2_ref[...] + e3 * o3_ref[...]) / (e1 + e2 + e3)
    gates = _sigmoid(gate_ref[...].astype(F32))
    bm = lambda a, w_ref: jnp.dot(a.astype(BF16), w_ref[...], preferred_element_type=F32)
    merged = (gates[:, 0:d] * bm(hg_ref[...], whg_ref)
              + gates[:, d:2 * d] * bm(rw_ref[...], wrw_ref)
              + gates[:, 2 * d:3 * d] * bm(att, watt_ref))
    xn = x_ref[...] + bm(merged, wout_ref)
    xo_ref[...] = xn
    ho_ref[...] = _rms(xn, gn_ref[...]).astype(ho_ref.dtype)


def merge_branches(x, c_gate, o_hg, o_rw, att_o, att_l, w_hg, w_rw, w_att, w_out, g_next):
    rows, d = x.shape
    tm = _row_tile(rows, 256)
    row = lambda a: pl.BlockSpec((tm, a.shape[1]), lambda i: (i, 0))
    full = lambda a: pl.BlockSpec(a.shape, lambda i: (0, 0))
    g2 = g_next.reshape(1, d)
    args = [x, c_gate, o_hg, o_rw, *att_o, *att_l]
    wts = [w_hg, w_rw, w_att, w_out, g2]
    return pl.pallas_call(
        _merge_body,
        grid=(rows // tm,),
        in_specs=[row(a) for a in args] + [full(a) for a in wts],
        out_specs=[pl.BlockSpec((tm, d), lambda i: (i, 0))] * 2,
        out_shape=[jax.ShapeDtypeStruct((rows, d), F32), jax.ShapeDtypeStruct((rows, d), BF16)],
        compiler_params=_params("parallel"),
        name="merge_branches",
    )(*args, *wts)


def _trunk_layer(x, h, lw, layer, batch, seq, state, last):
    d = x.shape[1]
    x, h = ffn_half_step(x, h, lw['ffn1_w_gu'], lw['ffn1_w_down'], lw['norm_mix'], BF16)
    c_hg = project(h, lw['w_in_hg'], 1024)
    c_rw = project(h, lw['w_in_rw'], lw['w_in_rw'].shape[1] // 2, BF16)
    c_att = project(h, lw['w_in_att'], 768)
    c_gate = project(h, lw['w_in_gate'], 1024, BF16)
    if state is None:
        o_hg, s_hg = hgrn_prompt(c_hg, lw['hg_params'], batch, seq)
        shift0 = jnp.zeros((batch, c_rw.shape[1]), F32)
        s_rw0 = jnp.zeros((batch, d // RW_HEAD, RW_HEAD, RW_HEAD), F32)
    else:
        s_hgrn, s_rwkv, s_shift, caches = state
        o_hg, s_hg = hgrn_sample(c_hg, lw['hg_params'], s_hgrn, layer, batch, seq)
        shift0, s_rw0 = s_shift[layer], s_rwkv[layer]
    o_rw, s_rw, shift = rwkv_branch(c_rw, shift0, s_rw0, lw, batch, seq)
    att_o, att_l, kv_rows = [], [], []
    c_att3 = c_att.reshape(batch, seq, -1)
    for gi, (win, dil) in enumerate(ATT_GROUPS):
        if state is None:
            o, l = att_prompt_group(c_att, gi, batch, seq)
        else:
            o, l = att_sample_group(c_att, caches[gi], layer, gi, batch, seq)
        att_o.append(o)
        att_l.append(l)
        keep = min(win, seq)
        kv = c_att3[:, seq - keep:, (3 * gi + 1) * ATT_WIDTH:(3 * gi + 3) * ATT_WIDTH]
        kv_rows.append(kv.reshape(batch, keep, 2, ATT_HEADS, ATT_HD))
    x, h = merge_branches(x, c_gate, o_hg, o_rw, att_o, att_l, lw['w_branch_hg'], lw['w_branch_rw'],
                          lw['w_branch_att'], lw['w_out'], lw['norm_ffn2'])
    x, h = ffn_half_step(x, h, lw['ffn2_w_gu'], lw['ffn2_w_down'], lw['norm_next'], F32 if last else BF16)
    return x, h, (s_hg, s_rw, shift, *kv_rows)


def _lower_bounds(hg_lb):
    p = jax.nn.softmax(hg_lb.astype(F32), axis=0)
    c = jnp.cumsum(p, axis=0)
    return c - c[0:1]


def kernel(x_prompt, x_sample, state_hgrn, state_rwkv, state_rwkv_shift, cache_att1_kv, cache_att2_kv, cache_att3_kv, norm_ffn1, ffn1_w_gu, ffn1_w_down, norm_mix, w_in, hg_lb, hg_gnorm, rw_mu, rw_w0, rw_w_up, rw_a0, rw_a_up, rw_g_up, rw_k_k, rw_k_a, rw_r_k, rw_lnx_w, rw_lnx_b, w_branch_hg, w_branch_rw, w_branch_att, w_out, norm_ffn2, ffn2_w_gu, ffn2_w_down, norm_final):
    depth = norm_ffn1.shape[0]
    bp, sp, d = x_prompt.shape
    bs, ss, _ = x_sample.shape
    hg_cols = 2 * HG_HEADS * HG_DK + 2 * hg_gnorm.shape[1]
    rw_cols = rw_mu.shape[1]
    att_cols = 3 * len(ATT_GROUPS) * ATT_WIDTH
    bounds = np.cumsum([0, hg_cols, rw_cols, att_cols, 3 * d])
    assert bounds[-1] == w_in.shape[2]
    lbs = _lower_bounds(hg_lb)
    layers = []
    for l in range(depth):
        lb = lbs[l]
        wl = w_in[l].astype(BF16)
        layers.append({
            'ffn1_w_gu': ffn1_w_gu[l].astype(BF16), 'ffn1_w_down': ffn1_w_down[l].astype(BF16),
            'ffn2_w_gu': ffn2_w_gu[l].astype(BF16), 'ffn2_w_down': ffn2_w_down[l].astype(BF16),
            'norm_mix': norm_mix[l], 'norm_ffn2': norm_ffn2[l],
            'norm_next': norm_ffn1[l + 1] if l + 1 < depth else norm_final,
            'w_in_hg': wl[:, bounds[0]:bounds[1]], 'w_in_rw': wl[:, bounds[1]:bounds[2]],
            'w_in_att': wl[:, bounds[2]:bounds[3]], 'w_in_gate': wl[:, bounds[3]:bounds[4]],
            'hg_params': jnp.stack([jnp.log(lb), jnp.log1p(-lb), 1.0 - lb, hg_gnorm[l]]),
            'rw_mu': rw_mu[l], 'rw_w0': rw_w0[l], 'rw_w_up': rw_w_up[l], 'rw_a0': rw_a0[l], 'rw_a_up': rw_a_up[l],
            'rw_g_up': rw_g_up[l], 'rw_k_k': rw_k_k[l], 'rw_k_a': rw_k_a[l], 'rw_r_k': rw_r_k[l],
            'rw_lnx_w': rw_lnx_w[l], 'rw_lnx_b': rw_lnx_b[l],
            'w_branch_hg': w_branch_hg[l].astype(BF16), 'w_branch_rw': w_branch_rw[l].astype(BF16),
            'w_branch_att': w_branch_att[l].astype(BF16), 'w_out': w_out[l].astype(BF16),
        })
    caches = (cache_att1_kv, cache_att2_kv, cache_att3_kv)
    sample_state = (state_hgrn, state_rwkv, state_rwkv_shift, caches)
    results = []
    for x3, batch, seq, state in ((x_prompt, bp, sp, None), (x_sample, bs, ss, sample_state)):
        x = x3.reshape(batch * seq, d)
        h = rmsnorm_rows(x, norm_ffn1[0], BF16)
        states = []
        for l in range(depth):
            x, h, st = _trunk_layer(x, h, layers[l], l, batch, seq, state, l + 1 == depth)
            states.append(st)
        stacked = [jnp.stack(z, axis=0) for z in zip(*states)]
        results.append((h.reshape(batch, seq, d), stacked))
    (yp, pst), (ys, sst) = results
    return (yp, ys, *pst, *sst)
```

```python
import functools
import math

import numpy as np
import jax
import jax.numpy as jnp
from jax import lax
from jax.experimental import pallas as pl
from jax.experimental.pallas import tpu as pltpu

F32 = jnp.float32
BF16 = jnp.bfloat16

HG_HEADS = 8
HG_DK = 128
RW_HEAD = 64
RW_LORA_W = 64
RW_LORA_A = 64
RW_LORA_G = 128
RW_LNX_EPS = 64e-5
ATT_GROUPS = ((128, 1), (512, 4), (2048, 16))
ATT_HEADS = 8
ATT_HD = 64
ATT_WIDTH = ATT_HEADS * ATT_HD
RMS_EPS = 1e-6
LANES = 128
SUBLANES = 8
MXU_DIM = 256
CHUNK = 64
HG_CHUNK = 128
VMEM_LIMIT = 48 * 1024 * 1024

NN = (((1,), (0,)), ((), ()))
NT = (((1,), (1,)), ((), ()))
TN = (((0,), (0,)), ((), ()))


def _dg(a, b, dims=NN):
    return lax.dot_general(a, b, dims, preferred_element_type=F32)


def _bdot(a, b, dims=NN):
    return lax.dot_general(a.astype(BF16), b.astype(BF16), dims, preferred_element_type=F32)


def _split2(x):
    hi = x.astype(BF16)
    lo = (x - hi.astype(F32)).astype(BF16)
    return hi, lo


def _split3(x):
    hi = x.astype(BF16)
    r = x - hi.astype(F32)
    mid = r.astype(BF16)
    lo = (r - mid.astype(F32)).astype(BF16)
    return hi, mid, lo


def _dot_sel(w, x, dims=NN):
    hi, mid, lo = _split3(x)
    return _dg(w, hi, dims) + _dg(w, mid, dims) + _dg(w, lo, dims)


def _dot_sel3(w3, x):
    return _dg(w3, jnp.concatenate(_split3(x), axis=0), NN)


def _dot_sel2(w2, x):
    return _dg(w2, jnp.concatenate(_split2(x), axis=0), NN)


def _dot_sel_rhs(x, w, dims=NN):
    hi, mid, lo = _split3(x)
    return _dg(hi, w, dims) + _dg(mid, w, dims) + _dg(lo, w, dims)


def _rms(x, g):
    return x * lax.rsqrt(jnp.mean(x * x, axis=-1, keepdims=True) + RMS_EPS) * g


def _sigmoid(x):
    return 1.0 / (1.0 + jnp.exp(-x))


def _params(*sem):
    return pltpu.CompilerParams(dimension_semantics=sem, vmem_limit_bytes=VMEM_LIMIT)


def _row_tile(rows, want):
    t = min(rows, want)
    assert rows % t == 0
    return t


def _rmsnorm_body(x_ref, g_ref, o_ref):
    o_ref[...] = _rms(x_ref[...], g_ref[...]).astype(o_ref.dtype)


def rmsnorm_rows(x, g, out_dtype):
    rows, d = x.shape
    tm = _row_tile(rows, 1024)
    return pl.pallas_call(
        _rmsnorm_body,
        grid=(rows // tm,),
        in_specs=[pl.BlockSpec((tm, d), lambda i: (i, 0)), pl.BlockSpec((1, d), lambda i: (0, 0))],
        out_specs=pl.BlockSpec((tm, d), lambda i: (i, 0)),
        out_shape=jax.ShapeDtypeStruct((rows, d), out_dtype),
        compiler_params=_params("parallel"),
        name="rmsnorm",
    )(x, g.reshape(1, d))


def _ffn_body(x_ref, h_ref, wg_ref, wu_ref, wd_ref, gn_ref, xo_ref, ho_ref, acc_ref):
    j = pl.program_id(1)

    @pl.when(j == 0)
    def _():
        acc_ref[...] = jnp.zeros_like(acc_ref)

    h = h_ref[...]
    gate = jnp.dot(h, wg_ref[...], preferred_element_type=F32)
    up = jnp.dot(h, wu_ref[...], preferred_element_type=F32)
    act = (gate * _sigmoid(gate) * up).astype(BF16)
    acc_ref[...] += jnp.dot(act, wd_ref[...], preferred_element_type=F32)

    @pl.when(j == pl.num_programs(1) - 1)
    def _():
        xn = x_ref[...] + 0.5 * acc_ref[...]
        xo_ref[...] = xn
        ho_ref[...] = _rms(xn, gn_ref[...]).astype(ho_ref.dtype)


def ffn_half_step(x, h, w_gu, w_down, g_next, next_dtype):
    rows, d = x.shape
    dff = w_down.shape[0]
    tm = _row_tile(rows, 1024)
    tf = 256
    nf = dff // tf
    assert dff % tf == 0
    return pl.pallas_call(
        _ffn_body,
        grid=(rows // tm, nf),
        in_specs=[
            pl.BlockSpec((tm, d), lambda i, j: (i, 0)),
            pl.BlockSpec((tm, d), lambda i, j: (i, 0)),
            pl.BlockSpec((d, tf), lambda i, j: (0, j)),
            pl.BlockSpec((d, tf), lambda i, j: (0, j + nf)),
            pl.BlockSpec((tf, d), lambda i, j: (j, 0)),
            pl.BlockSpec((1, d), lambda i, j: (0, 0)),
        ],
        out_specs=[pl.BlockSpec((tm, d), lambda i, j: (i, 0)), pl.BlockSpec((tm, d), lambda i, j: (i, 0))],
        out_shape=[jax.ShapeDtypeStruct((rows, d), F32), jax.ShapeDtypeStruct((rows, d), next_dtype)],
        scratch_shapes=[pltpu.VMEM((tm, d), F32)],
        compiler_params=_params("parallel", "arbitrary"),
        name="ffn_half_step",
    )(x, h, w_gu, w_gu, w_down, g_next.reshape(1, d))


def _proj_body(h_ref, w_ref, o_ref):
    o_ref[...] = jnp.dot(h_ref[...], w_ref[...], preferred_element_type=F32).astype(o_ref.dtype)


def project(h, w, tn, out_dtype=F32):
    rows, d = h.shape
    n = w.shape[1]
    tm = _row_tile(rows, 2048)
    assert n % tn == 0
    return pl.pallas_call(
        _proj_body,
        grid=(rows // tm, n // tn),
        in_specs=[pl.BlockSpec((tm, d), lambda i, j: (i, 0)), pl.BlockSpec((d, tn), lambda i, j: (0, j))],
        out_specs=pl.BlockSpec((tm, tn), lambda i, j: (i, j)),
        out_shape=jax.ShapeDtypeStruct((rows, n), out_dtype),
        compiler_params=_params("parallel", "parallel"),
        name="project",
    )(h, w)


def _level_constants(rows, seq):
    levels = int(math.log2(seq))
    assert 2 ** levels == seq and rows % seq == 0
    t = np.arange(rows)
    r = t[None, :]
    ws, ms = [], [np.eye(rows, dtype=bool)]
    for l in range(1, levels + 1):
        m = 2 ** l
        hm = m // 2
        par, pos = t // m, t % m
        sec = pos >= hm
        ref = par * m + hm - 1
        w = np.where(sec[:, None], (r > ref[:, None]) & (r <= t[:, None]), (r > t[:, None]) & (r <= ref[:, None]))
        ws.append(w)
        ms.append((par[:, None] == par[None, :]) & sec[:, None] & (~sec)[None, :])
    same = (t // seq)[:, None] == (t // seq)[None, :]
    ws.append(same & (r <= t[:, None]))
    ws.append(same & (r > t[:, None]))
    wmat = np.concatenate(ws, 0).astype(np.float32)
    return np.concatenate([wmat] * 2, axis=1), np.stack(ms).astype(np.float32)


def _hgrn_gates(f, loglb, log1mlb, onemlb):
    ls = jnp.minimum(f, 0.0) - jnp.log(1.0 + jnp.exp(-jnp.abs(f)))
    a = log1mlb + ls
    log_f = jnp.maximum(a, loglb) + jnp.log(1.0 + jnp.exp(-jnp.abs(a - loglb)))
    key = onemlb * jnp.exp(ls - f)
    return -log_f, key


def _hgrn_intra(q, k, e_lv, masks, nlev):
    a = _dg(q, k, NT) * masks[0]
    for l in range(1, nlev + 1):
        e = e_lv[l - 1]
        a = a + _dg(q * e, k * e, NT) * masks[l]
    return a


def _hgrn_prompt_body(q_ref, f_ref, i_ref, g_ref, lb_ref, w_ref, m_ref, o_ref, s_ref, st_ref, *, nlev):
    c = pl.program_id(1)
    rows = q_ref.shape[0]

    @pl.when(c == 0)
    def _():
        st_ref[...] = jnp.zeros_like(st_ref)

    nlf, key = _hgrn_gates(f_ref[...], lb_ref[0:1, :], lb_ref[1:2, :], lb_ref[2:3, :])
    dec = jnp.exp(-_dot_sel2(w_ref[...], nlf))
    qraw = q_ref[...]
    qact = qraw * _sigmoid(qraw)
    val = i_ref[...]
    og = g_ref[...]
    masks = [m_ref[l] for l in range(nlev + 1)]
    for h in range(HG_HEADS):
        sl = slice(h * HG_DK, (h + 1) * HG_DK)
        q, k, v = qact[:, sl], key[:, sl], val[:, sl]
        e_lv = [dec[l * rows:(l + 1) * rows, sl] for l in range(nlev)]
        e_cum = dec[nlev * rows:(nlev + 1) * rows, sl]
        e_suf = dec[(nlev + 1) * rows:(nlev + 2) * rows, sl]
        a = _hgrn_intra(q, k, e_lv, masks, nlev)
        st = st_ref[h]
        o = _dg(q * e_cum, st, NT) + _dg(a, v, NN)
        st_new = st * e_cum[rows - 1:rows, :] + _dg(v, k * e_suf, TN)
        st_ref[h] = st_new
        y = _rms(o, lb_ref[3:4, sl])
        ogh = og[:, sl]
        o_ref[:, sl] = (y * (ogh * _sigmoid(ogh))).astype(o_ref.dtype)

    @pl.when(c == pl.num_programs(1) - 1)
    def _():
        for h in range(HG_HEADS):
            s_ref[0, h] = st_ref[h].T


def hgrn_prompt(c_hg, lbp, batch, seq):
    width = c_hg.shape[1] // 4
    rows = min(seq, HG_CHUNK)
    nlev = int(math.log2(rows))
    wmat, masks = _level_constants(rows, rows)
    nc = seq // rows
    col = lambda j: pl.BlockSpec((rows, width), lambda b, c: (b * nc + c, j))
    return pl.pallas_call(
        functools.partial(_hgrn_prompt_body, nlev=nlev),
        grid=(batch, nc),
        in_specs=[col(0), col(1), col(2), col(3),
                  pl.BlockSpec(lbp.shape, lambda b, c: (0, 0)),
                  pl.BlockSpec(wmat.shape, lambda b, c: (0, 0)),
                  pl.BlockSpec(masks.shape, lambda b, c: (0, 0, 0))],
        out_specs=[pl.BlockSpec((rows, width), lambda b, c: (b * nc + c, 0)),
                   pl.BlockSpec((1, HG_HEADS, HG_DK, HG_DK), lambda b, c: (b, 0, 0, 0))],
        out_shape=[jax.ShapeDtypeStruct((batch * seq, width), BF16),
                   jax.ShapeDtypeStruct((batch, HG_HEADS, HG_DK, HG_DK), F32)],
        scratch_shapes=[pltpu.VMEM((HG_HEADS, HG_DK, HG_DK), F32)],
        compiler_params=_params("parallel", "arbitrary"),
        name="hgrn_prompt",
    )(c_hg, c_hg, c_hg, c_hg, lbp, jnp.asarray(wmat, BF16), jnp.asarray(masks))


def _hgrn_sample_body(q_ref, f_ref, i_ref, g_ref, lb_ref, w_ref, m_ref, s0_ref, o_ref, s_ref, *, nlev, seq):
    rows = q_ref.shape[0]
    nseq = rows // seq
    nlf, key = _hgrn_gates(f_ref[...], lb_ref[0:1, :], lb_ref[1:2, :], lb_ref[2:3, :])
    dec = jnp.exp(-_dot_sel2(w_ref[...], nlf))
    qraw = q_ref[...]
    q = qraw * _sigmoid(qraw)
    v = i_ref[...]
    og = g_ref[...]
    masks = [m_ref[l] for l in range(nlev + 1)]
    e_lv = [dec[l * rows:(l + 1) * rows] for l in range(nlev)]
    e_cum = dec[nlev * rows:(nlev + 1) * rows]
    e_suf = dec[(nlev + 1) * rows:(nlev + 2) * rows]
    a = _hgrn_intra(q, key, e_lv, masks, nlev)
    o_intra = _dg(a, v, NN)
    qc = q * e_cum
    ks = key * e_suf
    ones = jnp.ones((seq, HG_DK), BF16)
    outs = []
    for b in range(nseq):
        rs = slice(b * seq, (b + 1) * seq)
        s0 = s0_ref[b, 0]
        outs.append(_dg(qc[rs], s0, NN))
        total = _dot_sel_rhs(nlf[rs], ones, TN)
        s_ref[b, 0] = jnp.exp(-total) * s0 + _dg(ks[rs], v[rs], TN)
    o = o_intra + jnp.concatenate(outs, axis=0)
    y = _rms(o, lb_ref[3:4, :])
    o_ref[...] = (y * (og * _sigmoid(og))).astype(o_ref.dtype)


def hgrn_sample(c_hg, lbp, state, layer, batch, seq):
    width = c_hg.shape[1] // 4
    nh = width // HG_DK
    rows = CHUNK
    nseq = rows // seq
    nlev = int(math.log2(seq))
    wmat, masks = _level_constants(rows, seq)
    col = lambda j: pl.BlockSpec((rows, HG_DK), lambda i, h: (i, j * nh + h))
    lbspec = pl.BlockSpec((lbp.shape[0], HG_DK), lambda i, h: (0, h))
    return pl.pallas_call(
        functools.partial(_hgrn_sample_body, nlev=nlev, seq=seq),
        grid=(batch // nseq, nh),
        in_specs=[col(0), col(1), col(2), col(3), lbspec,
                  pl.BlockSpec(wmat.shape, lambda i, h: (0, 0)),
                  pl.BlockSpec(masks.shape, lambda i, h: (0, 0, 0)),
                  pl.BlockSpec((None, nseq, 1, HG_DK, HG_DK), lambda i, h: (layer, i, h, 0, 0))],
        out_specs=[pl.BlockSpec((rows, HG_DK), lambda i, h: (i, h)),
                   pl.BlockSpec((nseq, 1, HG_DK, HG_DK), lambda i, h: (i, h, 0, 0))],
        out_shape=[jax.ShapeDtypeStruct((batch * seq, width), BF16),
                   jax.ShapeDtypeStruct((batch, nh, HG_DK, HG_DK), F32)],
        compiler_params=_params("parallel", "parallel"),
        name="hgrn_sample",
    )(c_hg, c_hg, c_hg, c_hg, lbp, jnp.asarray(wmat, BF16), jnp.asarray(masks), state)


def _group_sum(x, bd):
    return _dg(jnp.concatenate(_split2(x), axis=1), bd, NN)


def _group_sum_wide(x, bd):
    return jnp.concatenate([_group_sum(x[:, j * LANES:(j + 1) * LANES], bd) for j in range(x.shape[1] // LANES)],
                           axis=1)


def _head_blockdiag(n, head):
    i = np.arange(n) // head
    bd = (i[:, None] == i[None, :]).astype(np.float32)
    return np.concatenate([bd, bd], axis=0)


def _rwkv_consts(rows, seq):
    t = np.arange(rows)
    sq = t // seq
    same = sq[:, None] == sq[None, :]
    r = t[None, :]
    pre = same & (r <= t[:, None])
    mid = sq * seq + seq // 2 - 1
    premid = same & (r <= mid[:, None])
    suf = same & (r > t[:, None])
    wmat = np.concatenate([pre.astype(np.float32) - premid.astype(np.float32), pre, suf], 0)
    t2 = np.concatenate([t, t])
    h2 = np.concatenate([0 * t, 0 * t + 1])
    s2 = t2 // seq
    blk = (h2[:, None] == h2[None, :]) & (s2[:, None] == s2[None, :])
    strict = blk & (t2[None, :] < t2[:, None])
    incl = blk & (t2[None, :] <= t2[:, None])
    lv = []
    for l in range(1, int(math.log2(seq)) + 1):
        m = 2 ** l
        par, pos = t2 // m, t2 % m
        lv.append(blk & (par[:, None] == par[None, :]) & (pos[:, None] >= m // 2) & (pos[None, :] < m // 2))
    masks = np.stack([strict, incl, np.eye(2 * rows, dtype=bool)] + lv).astype(np.float32)
    first = (t[:, None] == (np.arange(rows // seq) * seq)[None, :]).astype(np.float32)
    return np.concatenate([wmat] * 3, axis=1), masks, first


def _stack2(x, m0, m1):
    return jnp.concatenate([x * m0, x * m1], axis=0)


def _rwkv_chunk_body(c_ref, cp_ref, sh_ref, mu_ref, vec_ref, wup_ref, aup_ref, gup_ref, bd_ref, w_ref, m_ref,
                     f_ref, r2_ref, y2_ref, bn_ref, g_ref, mx_ref, z_ref, *, seq, nchunk):
    rows = c_ref.shape[0]
    w = r2_ref.shape[1]
    npair = w // LANES
    nseq = rows // seq
    nlev = int(math.log2(seq))

    c = c_ref[...].astype(F32)
    rolled = pltpu.roll(c, 1, 0)
    rowi = lax.broadcasted_iota(jnp.int32, (rows, 1), 0)
    if nseq == 1:
        last = cp_ref.shape[0] - 1
        head_row = jnp.where(pl.program_id(0) % nchunk == 0, sh_ref[...], cp_ref[last:last + 1, :].astype(F32))
        prev = jnp.where(rowi == 0, head_row, rolled)
    else:
        prev = jnp.where(rowi % seq == 0, _dot_sel(f_ref[...], sh_ref[...]), rolled)
    xs = c + mu_ref[...] * (prev - c)

    rr, k, vv = xs[:, 0:w], xs[:, w:2 * w], xs[:, 2 * w:3 * w]
    lora = xs[:, 3 * w:3 * w + RW_LORA_W + RW_LORA_A]
    gl = xs[:, 3 * w + RW_LORA_W + RW_LORA_A:]
    w0, a0, k_k, k_a, r_k = (vec_ref[i:i + 1, :] for i in range(5))
    lw = -math.exp(-0.5) * _sigmoid(w0 + _dg(jnp.tanh(lora), wup_ref[...]))
    a = _sigmoid(a0 + _dg(lora, aup_ref[...]))
    g_ref[...] = _dg(_sigmoid(gl), gup_ref[...]).astype(g_ref.dtype)
    bd = bd_ref[...]
    kkr = k * k_k
    kk = kkr * lax.rsqrt(jnp.maximum(_group_sum_wide(kkr * kkr, bd), 1e-24))
    kx = k * (1.0 + (a - 1.0) * k_a)
    bn_ref[...] = (_group_sum_wide(rr * kx * r_k, bd) * vv).astype(bn_ref.dtype)
    bb = kk * a
    aa = -kk

    d = _dot_sel3(w_ref[...], lw)
    dm, gin, suf = d[0:rows], d[rows:2 * rows], d[2 * rows:3 * rows]
    e_inv = jnp.exp(-dm)
    e_abs = jnp.exp(gin)
    e_suf = jnp.exp(suf)
    a_mid = aa * jnp.exp(dm - lw)
    r_mid = rr * jnp.exp(dm)
    b_inv = bb * e_inv
    k_inv = kx * e_inv
    a_abs = aa * jnp.exp(gin - lw)
    r_abs = rr * e_abs
    b_end = bb * e_suf
    k_end = kx * e_suf

    lane = lax.broadcasted_iota(jnp.int32, (rows, LANES), 1)
    m0 = (lane < RW_HEAD).astype(F32)
    m1 = 1.0 - m0
    masks = [m_ref[i] for i in range(3 + nlev)]
    n2 = 2 * rows
    pairs = range(npair)
    s2 = lambda x, p: _stack2(x[:, p * LANES:(p + 1) * LANES], m0, m1)

    big = [_bdot(jnp.concatenate([s2(a_mid, p), s2(r_mid, p)], axis=0),
                 jnp.concatenate([s2(b_inv, p), s2(k_inv, p)], axis=0), NT) for p in pairs]
    nmat = [big[p][0:n2, 0:n2] * masks[0] for p in pairs]
    tinv = [masks[2] + nmat[p] * masks[3] for p in pairs]
    for l in range(2, nlev + 1):
        tinv = [tinv[p] + _bdot(_bdot(tinv[p], nmat[p] * masks[2 + l]), tinv[p]) for p in pairs]
    v_st = [s2(vv, p) for p in pairs]
    cmask = jnp.concatenate([masks[0], masks[1]], axis=0)
    kv = [_bdot(big[p][:, n2:] * cmask, v_st[p]) for p in pairs]
    pq = [_bdot(tinv[p], jnp.concatenate([s2(a_abs, p), kv[p][0:n2]], axis=1)) for p in pairs]
    ry = [_bdot(big[p][n2:, 0:n2] * masks[1], pq[p]) for p in pairs]
    eye2 = masks[2][0:RW_HEAD, 0:LANES] + masks[2][RW_HEAD:LANES, 0:LANES]
    for p in pairs:
        sl = slice(p * LANES, (p + 1) * LANES)
        r2 = s2(r_abs, p) + ry[p][:, 0:LANES]
        y2 = ry[p][:, LANES:] + kv[p][n2:]
        r2_ref[:, sl] = (r2[0:rows] + r2[rows:]).astype(r2_ref.dtype)
        y2_ref[:, sl] = (y2[0:rows] + y2[rows:]).astype(y2_ref.dtype)
        b_st = s2(b_end, p)
        k_st = s2(k_end, p)
        for b in range(nseq):
            pick = lambda x: jnp.concatenate([x[b * seq:(b + 1) * seq], x[rows + b * seq:rows + (b + 1) * seq]], axis=0)
            mz = _bdot(pick(pq[p]), pick(b_st), TN)
            zz = mz[LANES:] + _bdot(pick(v_st[p]), pick(k_st), TN)
            g_end = e_abs[b * seq + seq - 1:b * seq + seq, sl]
            hs = slice(b * RW_HEAD, (b + 1) * RW_HEAD)
            mx_ref[hs, sl] = (mz[0:RW_HEAD] + mz[RW_HEAD:LANES] + eye2 * g_end).astype(mx_ref.dtype)
            z_ref[hs, sl] = (zz[0:RW_HEAD] + zz[RW_HEAD:]).astype(z_ref.dtype)


def rwkv_chunk(c_rw, shift0, lp, batch, seq):
    rows_all, ncol = c_rw.shape
    w = lp['rw_w0'].shape[0]
    rows = CHUNK
    cseq = min(seq, rows)
    nseq = rows // cseq
    nchunk = seq // cseq
    wmat, masks, first = _rwkv_consts(rows, cseq)
    vecs = jnp.stack([lp['rw_w0'], lp['rw_a0'], lp['rw_k_k'], lp['rw_k_a'], lp['rw_r_k'].reshape(w)])
    wup = jnp.concatenate([lp['rw_w_up'], jnp.zeros_like(lp['rw_a_up'])], axis=0)
    aup = jnp.concatenate([jnp.zeros_like(lp['rw_w_up']), lp['rw_a_up']], axis=0)
    bd = jnp.asarray(_head_blockdiag(LANES, RW_HEAD), BF16)
    mu2 = lp['rw_mu'].reshape(1, ncol)
    if nseq == 1:
        sh = shift0.reshape(batch, 1, ncol)
        sh_spec = pl.BlockSpec((None, 1, ncol), lambda i: (i // nchunk, 0, 0))
    else:
        sh = shift0
        sh_spec = pl.BlockSpec((nseq, ncol), lambda i: (i, 0))
    consts = [mu2, vecs, wup, aup, lp['rw_g_up'], bd, jnp.asarray(wmat, BF16), jnp.asarray(masks),
              jnp.asarray(first, BF16)]
    full = lambda a: pl.BlockSpec(a.shape, lambda i: (0,) * a.ndim)
    row = pl.BlockSpec((rows, w), lambda i: (i, 0))
    st = pl.BlockSpec((nseq * RW_HEAD, w), lambda i: (i, 0))
    nst = rows_all // cseq * RW_HEAD
    pblk = SUBLANES * (4 // c_rw.dtype.itemsize)
    per = rows // pblk
    return pl.pallas_call(
        functools.partial(_rwkv_chunk_body, seq=cseq, nchunk=nchunk),
        grid=(rows_all // rows,),
        in_specs=[pl.BlockSpec((rows, ncol), lambda i: (i, 0)),
                  pl.BlockSpec((pblk, ncol), lambda i: (jnp.maximum(i * per - 1, 0), 0)),
                  sh_spec] + [full(a) for a in consts],
        out_specs=[row, row, row, row, st, st],
        out_shape=[jax.ShapeDtypeStruct((rows_all, w), BF16)] * 4 + [jax.ShapeDtypeStruct((nst, w), BF16)] * 2,
        compiler_params=_params("parallel"),
        name="rwkv_chunk",
    )(c_rw, c_rw, sh, *consts)


def _rwkv_scan_body(r2_ref, y2_ref, mx_ref, z_ref, s0_ref, bn_ref, g_ref, vec_ref, bd_ref,
                    o_ref, s_ref, st_ref, *, seq):
    c = pl.program_id(1)
    rows = r2_ref.shape[0]
    nunit = rows // seq
    nstate = st_ref.shape[0]
    w = r2_ref.shape[1]

    @pl.when(c == 0)
    def _():
        st_ref[...] = s0_ref[...]

    lane = lax.broadcasted_iota(jnp.int32, (RW_HEAD, LANES), 1)
    m0 = (lane < RW_HEAD).astype(F32)
    m1 = 1.0 - m0
    ys = []
    for u in range(nunit):
        rs = slice(u * seq, (u + 1) * seq)
        hs = slice(u * RW_HEAD, (u + 1) * RW_HEAD)
        b = u if nstate > 1 else 0
        yb = []
        for p in range(w // LANES):
            sl = slice(p * LANES, (p + 1) * LANES)
            s = st_ref[b, :, sl]
            yb.append(_bdot(r2_ref[rs, sl], _stack2(s, m0, m1), NT) + y2_ref[rs, sl].astype(F32))
            st_ref[b, :, sl] = (_bdot(s, _stack2(mx_ref[hs, sl].astype(F32), m0, m1), NN)
                                + z_ref[hs, sl].astype(F32))
        ys.append(jnp.concatenate(yb, axis=1))
    y = jnp.concatenate(ys, axis=0) if nunit > 1 else ys[0]
    bd = bd_ref[...]
    mean = _group_sum_wide(y, bd) * (1.0 / RW_HEAD)
    yc = y - mean
    var = _group_sum_wide(yc * yc, bd) * (1.0 / RW_HEAD)
    yn = yc * lax.rsqrt(var + RW_LNX_EPS) * vec_ref[0:1, :] + vec_ref[1:2, :]
    o_ref[...] = ((yn + bn_ref[...]) * g_ref[...]).astype(o_ref.dtype)

    @pl.when(c == pl.num_programs(1) - 1)
    def _():
        s_ref[...] = st_ref[...]


def rwkv_scan(r2, y2, mx, z, s0, bonus, g, vecs, batch, seq):
    rows_all, w = r2.shape
    cseq = min(seq, CHUNK)
    if seq > CHUNK:
        rows, nseq = min(seq, 2 * CHUNK), 1
    else:
        rows, nseq = CHUNK, CHUNK // cseq
    nunit = rows // cseq
    nc = max(seq // rows, 1)
    bd = jnp.asarray(_head_blockdiag(LANES, RW_HEAD), BF16)
    row = pl.BlockSpec((rows, w), lambda i, c: (i * nc + c, 0))
    st = pl.BlockSpec((nunit * RW_HEAD, w), lambda i, c: (i * nc + c, 0))
    sblk = pl.BlockSpec((nseq, RW_HEAD, w), lambda i, c: (i, 0, 0))
    return pl.pallas_call(
        functools.partial(_rwkv_scan_body, seq=cseq),
        grid=(batch // nseq, nc),
        in_specs=[row, row, st, st, sblk, row, row,
                  pl.BlockSpec(vecs.shape, lambda i, c: (0, 0)), pl.BlockSpec(bd.shape, lambda i, c: (0, 0))],
        out_specs=[row, sblk],
        out_shape=[jax.ShapeDtypeStruct((rows_all, w), BF16), jax.ShapeDtypeStruct(s0.shape, F32)],
        scratch_shapes=[pltpu.VMEM((nseq, RW_HEAD, w), F32)],
        compiler_params=_params("parallel", "arbitrary"),
        name="rwkv_scan",
    )(r2, y2, mx, z, s0, bonus, g, vecs, bd)


def rwkv_branch(c_rw, shift0, s0, lp, batch, seq):
    ncol = c_rw.shape[1]
    w = lp['rw_w0'].shape[0]
    nh = w // RW_HEAD
    r2, y2, bonus, g, mx, z = rwkv_chunk(c_rw, shift0, lp, batch, seq)
    s0t = jnp.transpose(s0, (0, 2, 1, 3)).reshape(batch, RW_HEAD, w)
    ovecs = jnp.stack([lp['rw_lnx_w'], lp['rw_lnx_b']])
    out, s_fin = rwkv_scan(r2, y2, mx, z, s0t, bonus, g, ovecs, batch, seq)
    s_fin = jnp.transpose(s_fin.reshape(batch, RW_HEAD, nh, RW_HEAD), (0, 2, 1, 3))
    return out, s_fin, c_rw.reshape(batch, seq, ncol)[:, -1].astype(F32)


def _att_prompt_body(q_ref, k_ref, v_ref, o_ref, l_ref, *, sub, dil):
    scale = ATT_HD ** -0.5
    nblk = q_ref.shape[0] // (sub * dil)
    qi = lax.broadcasted_iota(jnp.int32, (sub, 2 * sub), 0)
    ki = lax.broadcasted_iota(jnp.int32, (sub, 2 * sub), 1) - sub
    dist = qi - ki
    valid_two = (dist >= 0) & (dist <= sub)
    valid_one = valid_two & (ki >= 0)
    lane = lax.broadcasted_iota(jnp.int32, (1, LANES), 1)
    hm = [(lane < ATT_HD).astype(F32), (lane >= ATT_HD).astype(F32)]
    for r in range(dil):
        kprev = vprev = None
        for j in range(nblk):
            rs = pl.ds(r + j * sub * dil, sub, stride=dil) if dil > 1 else pl.ds(j * sub, sub)
            q = q_ref[rs, :] * scale
            kcur = k_ref[rs, :].astype(BF16)
            vcur = v_ref[rs, :].astype(BF16)
            vcur = jnp.concatenate([vcur, jnp.ones_like(vcur)], axis=1)
            if j == 0:
                kprev, vprev, valid = kcur, vcur, valid_one
            else:
                valid = valid_two
            kcat = jnp.concatenate([kprev, kcur], axis=0)
            vone = jnp.concatenate([vprev, vcur], axis=0)
            kprev, vprev = kcur, vcur
            o = jnp.zeros((sub, LANES), F32)
            lse = jnp.zeros((sub, LANES), F32)
            for m in hm:
                s = _dg((q * m).astype(BF16), kcat, NT)
                s = jnp.where(valid, s, -jnp.inf)
                mx = jnp.max(s, axis=-1, keepdims=True)
                pr = jnp.exp((s - mx).astype(BF16))
                od = _dg(pr, vone, NN)
                den = od[:, LANES:]
                o = o + od[:, 0:LANES] * (m / den)
                lse = lse + (mx + jnp.log(den)) * m
            o_ref[rs, :] = o
            l_ref[rs, :] = lse


def att_prompt_group(c_att, gi, batch, seq):
    win, dil = ATT_GROUPS[gi]
    sub = win // dil
    assert seq % win == 0
    ppw = ATT_WIDTH // LANES
    col = lambda j: pl.BlockSpec((seq, LANES), lambda b, p: (b, (3 * gi + j) * ppw + p))
    outb = pl.BlockSpec((seq, LANES), lambda b, p: (b, p))
    return pl.pallas_call(
        functools.partial(_att_prompt_body, sub=sub, dil=dil),
        grid=(batch, ppw),
        in_specs=[col(0), col(1), col(2)],
        out_specs=[outb, outb],
        out_shape=[jax.ShapeDtypeStruct((batch * seq, ATT_WIDTH), F32)] * 2,
        compiler_params=_params("parallel", "parallel"),
        name="att_prompt",
    )(c_att, c_att, c_att)


def _att_sample_body(q_ref, kn_ref, vn_ref, c_ref, bias_ref, o_ref, l_ref, *, dil, seq):
    hpq = MXU_DIM // ATT_HD
    trow = lax.broadcasted_iota(jnp.int32, (hpq * seq, seq), 0) % seq
    tcol = lax.broadcasted_iota(jnp.int32, (hpq * seq, seq), 1)
    new_ok = (tcol <= trow) & ((trow - tcol) % dil == 0)
    lane_head = lax.broadcasted_iota(jnp.int32, (seq, MXU_DIM), 1) // ATT_HD
    hm = [(lane_head == h).astype(F32) for h in range(hpq)]
    bias = bias_ref[...]
    for b in range(c_ref.shape[0]):
        rs = slice(b * seq, (b + 1) * seq)
        q = q_ref[rs, :] * (ATT_HD ** -0.5)
        kn, vn = kn_ref[rs, :], vn_ref[rs, :]
        for g in range(ATT_WIDTH // MXU_DIM):
            cs = slice(g * MXU_DIM, (g + 1) * MXU_DIM)
            qst = jnp.concatenate([q[:, cs] * m for m in hm], axis=0)
            sc = _bdot(qst, c_ref[b, 0, cs, :], NN) + bias
            sn = jnp.where(new_ok, _bdot(qst, kn[:, cs], NT), -jnp.inf)
            mx = jnp.maximum(jnp.max(sc, axis=-1, keepdims=True), jnp.max(sn, axis=-1, keepdims=True))
            pc = jnp.exp(sc - mx)
            pn = jnp.exp(sn - mx)
            den = jnp.sum(pc, axis=-1, keepdims=True) + jnp.sum(pn, axis=-1, keepdims=True)
            o = (_bdot(pc, c_ref[b, 1, cs, :], NT) + _bdot(pn, vn[:, cs], NN)) / den
            lse = mx + jnp.log(den)
            o_ref[rs, cs] = sum(o[h * seq:(h + 1) * seq] * hm[h] for h in range(hpq))
            l_ref[rs, cs] = sum(lse[h * seq:(h + 1) * seq] * hm[h] for h in range(hpq))


def att_sample_group(c_att, cache, layer, gi, batch, seq):
    win, dil = ATT_GROUPS[gi]
    depth, _, length = cache.shape[:3]
    assert length == win and seq <= win // dil
    cv = jnp.transpose(cache, (0, 1, 3, 4, 5, 2)).reshape(depth, batch, 2, ATT_WIDTH, length)
    hpq = MXU_DIM // ATT_HD
    t = np.tile(np.arange(seq), hpq)[:, None]
    j = np.arange(length)[None, :]
    bias = jnp.asarray(np.where((j >= t) & ((j - t) % dil == 0), 0.0, -np.inf), F32)
    nb = max(1, min(8, ATT_GROUPS[-1][0] // length))
    assert batch % nb == 0
    new = lambda c: pl.BlockSpec((nb * seq, ATT_WIDTH), lambda b: (b, 3 * gi + c))
    outb = pl.BlockSpec((nb * seq, ATT_WIDTH), lambda b: (b, 0))
    return pl.pallas_call(
        functools.partial(_att_sample_body, dil=dil, seq=seq),
        grid=(batch // nb,),
        in_specs=[new(0), new(1), new(2),
                  pl.BlockSpec((None, nb, 2, ATT_WIDTH, length), lambda b: (layer, b, 0, 0, 0)),
                  pl.BlockSpec(bias.shape, lambda b: (0, 0))],
        out_specs=[outb, outb],
        out_shape=[jax.ShapeDtypeStruct((batch * seq, ATT_WIDTH), F32)] * 2,
        compiler_params=_params("parallel"),
        name="att_sample",
    )(c_att, c_att, c_att, cv, bias)


def _merge_body(x_ref, gate_ref, hg_ref, rw_ref, o1_ref, o2_ref, o3_ref, l1_ref, l2_ref, l3_ref,
                whg_ref, wrw_ref, watt_ref, wout_ref, gn_ref, xo_ref, ho_ref):
    d = x_ref.shape[1]
    l1, l2, l3 = l1_ref[...], l2_ref[...], l3_ref[...]
    mx = jnp.maximum(jnp.maximum(l1, l2), l3)
    e1, e2, e3 = jnp.exp(l1 - mx), jnp.exp(l2 - mx), jnp.exp(l3 - mx)
    att = (e1 * o1_ref[...] + e2 * o2_ref[...] + e3 * o3_ref[...]) / (e1 + e2 + e3)
    gates = _sigmoid(gate_ref[...].astype(F32))
    bm = lambda a, w_ref: jnp.dot(a.astype(BF16), w_ref[...], preferred_element_type=F32)
    merged = (gates[:, 0:d] * bm(hg_ref[...], whg_ref)
              + gates[:, d:2 * d] * bm(rw_ref[...], wrw_ref)
              + gates[:, 2 * d:3 * d] * bm(att, watt_ref))
    xn = x_ref[...] + bm(merged, wout_ref)
    xo_ref[...] = xn
    ho_ref[...] = _rms(xn, gn_ref[...]).astype(ho_ref.dtype)


def merge_branches(x, c_gate, o_hg, o_rw, att_o, att_l, w_hg, w_rw, w_att, w_out, g_next):
    rows, d = x.shape
    tm = _row_tile(rows, 256)
    row = lambda a: pl.BlockSpec((tm, a.shape[1]), lambda i: (i, 0))
    full = lambda a: pl.BlockSpec(a.shape, lambda i: (0, 0))
    g2 = g_next.reshape(1, d)
    args = [x, c_gate, o_hg, o_rw, *att_o, *att_l]
    wts = [w_hg, w_rw, w_att, w_out, g2]
    return pl.pallas_call(
        _merge_body,
        grid=(rows // tm,),
        in_specs=[row(a) for a in args] + [full(a) for a in wts],
        out_specs=[pl.BlockSpec((tm, d), lambda i: (i, 0))] * 2,
        out_shape=[jax.ShapeDtypeStruct((rows, d), F32), jax.ShapeDtypeStruct((rows, d), BF16)],
        compiler_params=_params("parallel"),
        name="merge_branches",
    )(*args, *wts)


def _trunk_layer(x, h, lw, layer, batch, seq, state, last):
    d = x.shape[1]
    x, h = ffn_half_step(x, h, lw['ffn1_w_gu'], lw['ffn1_w_down'], lw['norm_mix'], BF16)
    c_hg = project(h, lw['w_in_hg'], 1024)
    c_rw = project(h, lw['w_in_rw'], lw['w_in_rw'].shape[1] // 2, BF16)
    c_att = project(h, lw['w_in_att'], 768)
    c_gate = project(h, lw['w_in_gate'], 1024, BF16)
    if state is None:
        o_hg, s_hg = hgrn_prompt(c_hg, lw['hg_params'], batch, seq)
        shift0 = jnp.zeros((batch, c_rw.shape[1]), F32)
        s_rw0 = jnp.zeros((batch, d // RW_HEAD, RW_HEAD, RW_HEAD), F32)
    else:
        s_hgrn, s_rwkv, s_shift, caches = state
        o_hg, s_hg = hgrn_sample(c_hg, lw['hg_params'], s_hgrn, layer, batch, seq)
        shift0, s_rw0 = s_shift[layer], s_rwkv[layer]
    o_rw, s_rw, shift = rwkv_branch(c_rw, shift0, s_rw0, lw, batch, seq)
    att_o, att_l, kv_rows = [], [], []
    c_att3 = c_att.reshape(batch, seq, -1)
    for gi, (win, dil) in enumerate(ATT_GROUPS):
        if state is None:
            o, l = att_prompt_group(c_att, gi, batch, seq)
        else:
            o, l = att_sample_group(c_att, caches[gi], layer, gi, batch, seq)
        att_o.append(o)
        att_l.append(l)
        keep = min(win, seq)
        kv = c_att3[:, seq - keep:, (3 * gi + 1) * ATT_WIDTH:(3 * gi + 3) * ATT_WIDTH]
        kv_rows.append(kv.reshape(batch, keep, 2, ATT_HEADS, ATT_HD))
    x, h = merge_branches(x, c_gate, o_hg, o_rw, att_o, att_l, lw['w_branch_hg'], lw['w_branch_rw'],
                          lw['w_branch_att'], lw['w_out'], lw['norm_ffn2'])
    x, h = ffn_half_step(x, h, lw['ffn2_w_gu'], lw['ffn2_w_down'], lw['norm_next'], F32 if last else BF16)
    return x, h, (s_hg, s_rw, shift, *kv_rows)


def _lower_bounds(hg_lb):
    p = jax.nn.softmax(hg_lb.astype(F32), axis=0)
    c = jnp.cumsum(p, axis=0)
    return c - c[0:1]


def kernel(x_prompt, x_sample, state_hgrn, state_rwkv, state_rwkv_shift, cache_att1_kv, cache_att2_kv, cache_att3_kv, norm_ffn1, ffn1_w_gu, ffn1_w_down, norm_mix, w_in, hg_lb, hg_gnorm, rw_mu, rw_w0, rw_w_up, rw_a0, rw_a_up, rw_g_up, rw_k_k, rw_k_a, rw_r_k, rw_lnx_w, rw_lnx_b, w_branch_hg, w_branch_rw, w_branch_att, w_out, norm_ffn2, ffn2_w_gu, ffn2_w_down, norm_final):
    depth = norm_ffn1.shape[0]
    bp, sp, d = x_prompt.shape
    bs, ss, _ = x_sample.shape
    hg_cols = 2 * HG_HEADS * HG_DK + 2 * hg_gnorm.shape[1]
    rw_cols = rw_mu.shape[1]
    att_cols = 3 * len(ATT_GROUPS) * ATT_WIDTH
    bounds = np.cumsum([0, hg_cols, rw_cols, att_cols, 3 * d])
    assert bounds[-1] == w_in.shape[2]
    lbs = _lower_bounds(hg_lb)
    layers = []
    for l in range(depth):
        lb = lbs[l]
        wl = w_in[l].astype(BF16)
        layers.append({
            'ffn1_w_gu': ffn1_w_gu[l].astype(BF16), 'ffn1_w_down': ffn1_w_down[l].astype(BF16),
            'ffn2_w_gu': ffn2_w_gu[l].astype(BF16), 'ffn2_w_down': ffn2_w_down[l].astype(BF16),
            'norm_mix': norm_mix[l], 'norm_ffn2': norm_ffn2[l],
            'norm_next': norm_ffn1[l + 1] if l + 1 < depth else norm_final,
            'w_in_hg': wl[:, bounds[0]:bounds[1]], 'w_in_rw': wl[:, bounds[1]:bounds[2]],
            'w_in_att': wl[:, bounds[2]:bounds[3]], 'w_in_gate': wl[:, bounds[3]:bounds[4]],
            'hg_params': jnp.stack([jnp.log(lb), jnp.log1p(-lb), 1.0 - lb, hg_gnorm[l]]),
            'rw_mu': rw_mu[l], 'rw_w0': rw_w0[l], 'rw_w_up': rw_w_up[l], 'rw_a0': rw_a0[l], 'rw_a_up': rw_a_up[l],
            'rw_g_up': rw_g_up[l], 'rw_k_k': rw_k_k[l], 'rw_k_a': rw_k_a[l], 'rw_r_k': rw_r_k[l],
            'rw_lnx_w': rw_lnx_w[l], 'rw_lnx_b': rw_lnx_b[l],
            'w_branch_hg': w_branch_hg[l].astype(BF16), 'w_branch_rw': w_branch_rw[l].astype(BF16),
            'w_branch_att': w_branch_att[l].astype(BF16), 'w_out': w_out[l].astype(BF16),
        })
    caches = (cache_att1_kv, cache_att2_kv, cache_att3_kv)
    sample_state = (state_hgrn, state_rwkv, state_rwkv_shift, caches)
    results = []
    for x3, batch, seq, state in ((x_prompt, bp, sp, None), (x_sample, bs, ss, sample_state)):
        x = x3.reshape(batch * seq, d)
        h = rmsnorm_rows(x, norm_ffn1[0], BF16)
        states = []
        for l in range(depth):
            x, h, st = _trunk_layer(x, h, layers[l], l, batch, seq, state, l + 1 == depth)
            states.append(st)
        stacked = [jnp.stack(z, axis=0) for z in zip(*states)]
        results.append((h.reshape(batch, seq, d), stacked))
    (yp, pst), (ys, sst) = results
    return (yp, ys, *pst, *sst)
```

```python
import functools
import math

import numpy as np
import jax
import jax.numpy as jnp
from jax import lax
from jax.experimental import pallas as pl
from jax.experimental.pallas import tpu as pltpu

F32 = jnp.float32
BF16 = jnp.bfloat16

HG_HEADS = 8
HG_DK = 128
RW_HEAD = 64
RW_LORA_W = 64
RW_LORA_A = 64
RW_LORA_G = 128
RW_LNX_EPS = 64e-5
ATT_GROUPS = ((128, 1), (512, 4), (2048, 16))
ATT_HEADS = 8
ATT_HD = 64
ATT_WIDTH = ATT_HEADS * ATT_HD
RMS_EPS = 1e-6
LANES = 128
SUBLANES = 8
MXU_DIM = 256
CHUNK = 64
HG_CHUNK = 128
VMEM_LIMIT = 48 * 1024 * 1024

NN = (((1,), (0,)), ((), ()))
NT = (((1,), (1,)), ((), ()))
TN = (((0,), (0,)), ((), ()))


def _dg(a, b, dims=NN):
    return lax.dot_general(a, b, dims, preferred_element_type=F32)


def _bdot(a, b, dims=NN):
    return lax.dot_general(a.astype(BF16), b.astype(BF16), dims, preferred_element_type=F32)


def _split2(x):
    hi = x.astype(BF16)
    lo = (x - hi.astype(F32)).astype(BF16)
    return hi, lo


def _split3(x):
    hi = x.astype(BF16)
    r = x - hi.astype(F32)
    mid = r.astype(BF16)
    lo = (r - mid.astype(F32)).astype(BF16)
    return hi, mid, lo


def _dot_sel(w, x, dims=NN):
    hi, mid, lo = _split3(x)
    return _dg(w, hi, dims) + _dg(w, mid, dims) + _dg(w, lo, dims)


def _dot_sel3(w3, x):
    return _dg(w3, jnp.concatenate(_split3(x), axis=0), NN)


def _dot_sel2(w2, x):
    return _dg(w2, jnp.concatenate(_split2(x), axis=0), NN)


def _column_sums(x, ones3):
    hi = x.astype(BF16).astype(F32)
    r = x - hi
    mid = r.astype(BF16).astype(F32)
    terms = jnp.concatenate([hi, mid, r - mid], axis=0).astype(BF16)
    return _dg(terms, ones3, TN)


def _rms(x, g):
    return x * lax.rsqrt(jnp.mean(x * x, axis=-1, keepdims=True) + RMS_EPS) * g


def _sigmoid(x):
    return 1.0 / (1.0 + jnp.exp(-x))


def _params(*sem):
    return pltpu.CompilerParams(dimension_semantics=sem, vmem_limit_bytes=VMEM_LIMIT)


def _row_tile(rows, want):
    t = min(rows, want)
    assert rows % t == 0
    return t


def _rmsnorm_body(x_ref, g_ref, o_ref):
    o_ref[...] = _rms(x_ref[...], g_ref[...]).astype(o_ref.dtype)


def rmsnorm_rows(x, g, out_dtype):
    rows, d = x.shape
    tm = _row_tile(rows, 1024)
    return pl.pallas_call(
        _rmsnorm_body,
        grid=(rows // tm,),
        in_specs=[pl.BlockSpec((tm, d), lambda i: (i, 0)), pl.BlockSpec((1, d), lambda i: (0, 0))],
        out_specs=pl.BlockSpec((tm, d), lambda i: (i, 0)),
        out_shape=jax.ShapeDtypeStruct((rows, d), out_dtype),
        compiler_params=_params("parallel"),
        name="rmsnorm",
    )(x, g.reshape(1, d))


def _ffn_body(x_ref, h_ref, wg_ref, wu_ref, wd_ref, gn_ref, xo_ref, ho_ref, acc_ref):
    j = pl.program_id(1)

    @pl.when(j == 0)
    def _():
        acc_ref[...] = jnp.zeros_like(acc_ref)

    h = h_ref[...]
    gate = jnp.dot(h, wg_ref[...], preferred_element_type=F32)
    up = jnp.dot(h, wu_ref[...], preferred_element_type=F32)
    act = (gate * _sigmoid(gate) * up).astype(BF16)
    acc_ref[...] += jnp.dot(act, wd_ref[...], preferred_element_type=F32)

    @pl.when(j == pl.num_programs(1) - 1)
    def _():
        xn = x_ref[...] + 0.5 * acc_ref[...]
        xo_ref[...] = xn
        ho_ref[...] = _rms(xn, gn_ref[...]).astype(ho_ref.dtype)


def ffn_half_step(x, h, w_gu, w_down, g_next, next_dtype):
    rows, d = x.shape
    dff = w_down.shape[0]
    tm = _row_tile(rows, 1024)
    tf = 256
    nf = dff // tf
    assert dff % tf == 0
    return pl.pallas_call(
        _ffn_body,
        grid=(rows // tm, nf),
        in_specs=[
            pl.BlockSpec((tm, d), lambda i, j: (i, 0)),
            pl.BlockSpec((tm, d), lambda i, j: (i, 0)),
            pl.BlockSpec((d, tf), lambda i, j: (0, j)),
            pl.BlockSpec((d, tf), lambda i, j: (0, j + nf)),
            pl.BlockSpec((tf, d), lambda i, j: (j, 0)),
            pl.BlockSpec((1, d), lambda i, j: (0, 0)),
        ],
        out_specs=[pl.BlockSpec((tm, d), lambda i, j: (i, 0)), pl.BlockSpec((tm, d), lambda i, j: (i, 0))],
        out_shape=[jax.ShapeDtypeStruct((rows, d), F32), jax.ShapeDtypeStruct((rows, d), next_dtype)],
        scratch_shapes=[pltpu.VMEM((tm, d), F32)],
        compiler_params=_params("parallel", "arbitrary"),
        name="ffn_half_step",
    )(x, h, w_gu, w_gu, w_down, g_next.reshape(1, d))


def _proj_body(h_ref, w_ref, o_ref):
    o_ref[...] = jnp.dot(h_ref[...], w_ref[...], preferred_element_type=F32).astype(o_ref.dtype)


def project(h, w, tn, out_dtype=F32):
    rows, d = h.shape
    n = w.shape[1]
    tm = _row_tile(rows, 2048)
    assert n % tn == 0
    return pl.pallas_call(
        _proj_body,
        grid=(rows // tm, n // tn),
        in_specs=[pl.BlockSpec((tm, d), lambda i, j: (i, 0)), pl.BlockSpec((d, tn), lambda i, j: (0, j))],
        out_specs=pl.BlockSpec((tm, tn), lambda i, j: (i, j)),
        out_shape=jax.ShapeDtypeStruct((rows, n), out_dtype),
        compiler_params=_params("parallel", "parallel"),
        name="project",
    )(h, w)


def _level_constants(rows, seq):
    levels = int(math.log2(seq))
    assert 2 ** levels == seq and rows % seq == 0
    t = np.arange(rows)
    r = t[None, :]
    ws, ms = [], [np.eye(rows, dtype=bool)]
    for l in range(1, levels + 1):
        m = 2 ** l
        hm = m // 2
        par, pos = t // m, t % m
        sec = pos >= hm
        ref = par * m + hm - 1
        w = np.where(sec[:, None], (r > ref[:, None]) & (r <= t[:, None]), (r > t[:, None]) & (r <= ref[:, None]))
        ws.append(w)
        ms.append((par[:, None] == par[None, :]) & sec[:, None] & (~sec)[None, :])
    same = (t // seq)[:, None] == (t // seq)[None, :]
    ws.append(same & (r <= t[:, None]))
    ws.append(same & (r > t[:, None]))
    wmat = np.concatenate(ws, 0).astype(np.float32)
    return np.concatenate([wmat] * 2, axis=1), np.stack(ms).astype(np.float32)


def _hgrn_gates(f, loglb, log1mlb, onemlb):
    ls = jnp.minimum(f, 0.0) - jnp.log(1.0 + jnp.exp(-jnp.abs(f)))
    a = log1mlb + ls
    log_f = jnp.maximum(a, loglb) + jnp.log(1.0 + jnp.exp(-jnp.abs(a - loglb)))
    key = onemlb * jnp.exp(ls - f)
    return -log_f, key


def _hgrn_intra(q, k, e_lv, masks, nlev):
    a = _dg(q, k, NT) * masks[0]
    for l in range(1, nlev + 1):
        e = e_lv[l - 1]
        a = a + _dg(q * e, k * e, NT) * masks[l]
    return a


def _hgrn_prompt_body(q_ref, f_ref, i_ref, g_ref, lb_ref, w_ref, m_ref, o_ref, s_ref, st_ref, *, nlev):
    c = pl.program_id(1)
    rows = q_ref.shape[0]

    @pl.when(c == 0)
    def _():
        st_ref[...] = jnp.zeros_like(st_ref)

    nlf, key = _hgrn_gates(f_ref[...], lb_ref[0:1, :], lb_ref[1:2, :], lb_ref[2:3, :])
    dec = jnp.exp(-_dot_sel2(w_ref[...], nlf))
    qraw = q_ref[...]
    qact = qraw * _sigmoid(qraw)
    val = i_ref[...]
    og = g_ref[...]
    masks = [m_ref[l] for l in range(nlev + 1)]
    mask2 = [jnp.concatenate([m, m], axis=1) for m in masks]
    a_all = []
    for hp in range(HG_HEADS // 2):
        s0 = slice(2 * hp * HG_DK, (2 * hp + 1) * HG_DK)
        s1 = slice((2 * hp + 1) * HG_DK, (2 * hp + 2) * HG_DK)
        s01 = slice(2 * hp * HG_DK, (2 * hp + 2) * HG_DK)
        zero = jnp.zeros((rows, HG_DK), F32)
        acc = None
        for l in range(nlev + 1):
            if l == 0:
                qe, k0, k1 = qact[:, s01], key[:, s0], key[:, s1]
            else:
                e = dec[(l - 1) * rows:l * rows]
                qe, k0, k1 = qact[:, s01] * e[:, s01], key[:, s0] * e[:, s0], key[:, s1] * e[:, s1]
            kbd = jnp.concatenate([jnp.concatenate([k0, zero], axis=1), jnp.concatenate([zero, k1], axis=1)], axis=0)
            term = _bdot(qe, kbd, NT) * mask2[l]
            acc = term if acc is None else acc + term
        a_all += [acc[:, 0:rows], acc[:, rows:2 * rows]]
    for h in range(HG_HEADS):
        sl = slice(h * HG_DK, (h + 1) * HG_DK)
        q, k, v = qact[:, sl], key[:, sl], val[:, sl]
        e_cum = dec[nlev * rows:(nlev + 1) * rows, sl]
        e_suf = dec[(nlev + 1) * rows:(nlev + 2) * rows, sl]
        a = a_all[h]
        st = st_ref[h]
        o = _dg(q * e_cum, st, NT) + _dg(a, v, NN)
        st_new = st * e_cum[rows - 1:rows, :] + _dg(v, k * e_suf, TN)
        st_ref[h] = st_new
        y = _rms(o, lb_ref[3:4, sl])
        ogh = og[:, sl]
        o_ref[:, sl] = (y * (ogh * _sigmoid(ogh))).astype(o_ref.dtype)

    @pl.when(c == pl.num_programs(1) - 1)
    def _():
        for h in range(HG_HEADS):
            s_ref[0, h] = st_ref[h].T


def hgrn_prompt(c_hg, lbp, batch, seq):
    width = c_hg.shape[1] // 4
    rows = min(seq, HG_CHUNK)
    nlev = int(math.log2(rows))
    wmat, masks = _level_constants(rows, rows)
    nc = seq // rows
    col = lambda j: pl.BlockSpec((rows, width), lambda b, c: (b * nc + c, j))
    return pl.pallas_call(
        functools.partial(_hgrn_prompt_body, nlev=nlev),
        grid=(batch, nc),
        in_specs=[col(0), col(1), col(2), col(3),
                  pl.BlockSpec(lbp.shape, lambda b, c: (0, 0)),
                  pl.BlockSpec(wmat.shape, lambda b, c: (0, 0)),
                  pl.BlockSpec(masks.shape, lambda b, c: (0, 0, 0))],
        out_specs=[pl.BlockSpec((rows, width), lambda b, c: (b * nc + c, 0)),
                   pl.BlockSpec((1, HG_HEADS, HG_DK, HG_DK), lambda b, c: (b, 0, 0, 0))],
        out_shape=[jax.ShapeDtypeStruct((batch * seq, width), BF16),
                   jax.ShapeDtypeStruct((batch, HG_HEADS, HG_DK, HG_DK), F32)],
        scratch_shapes=[pltpu.VMEM((HG_HEADS, HG_DK, HG_DK), F32)],
        compiler_params=_params("parallel", "arbitrary"),
        name="hgrn_prompt",
    )(c_hg, c_hg, c_hg, c_hg, lbp, jnp.asarray(wmat, BF16), jnp.asarray(masks))


def _hgrn_sample_body(q_ref, f_ref, i_ref, g_ref, lb_ref, w_ref, m_ref, s0_ref, o_ref, s_ref, *, nlev, seq):
    rows = q_ref.shape[0]
    nseq = rows // seq
    nlf, key = _hgrn_gates(f_ref[...], lb_ref[0:1, :], lb_ref[1:2, :], lb_ref[2:3, :])
    dec = jnp.exp(-_dot_sel2(w_ref[...], nlf))
    qraw = q_ref[...]
    q = qraw * _sigmoid(qraw)
    v = i_ref[...]
    og = g_ref[...]
    masks = [m_ref[l] for l in range(nlev + 1)]
    e_lv = [dec[l * rows:(l + 1) * rows] for l in range(nlev)]
    e_cum = dec[nlev * rows:(nlev + 1) * rows]
    e_suf = dec[(nlev + 1) * rows:(nlev + 2) * rows]
    a = _hgrn_intra(q, key, e_lv, masks, nlev)
    o_intra = _dg(a, v, NN)
    qc = q * e_cum
    ks = key * e_suf
    ones = jnp.ones((3 * seq, HG_DK), BF16)
    outs = []
    for b in range(nseq):
        rs = slice(b * seq, (b + 1) * seq)
        s0 = s0_ref[b, 0]
        outs.append(_dg(qc[rs], s0, NN))
        total = _column_sums(nlf[rs], ones)
        s_ref[b, 0] = jnp.exp(-total) * s0 + _dg(ks[rs], v[rs], TN)
    o = o_intra + jnp.concatenate(outs, axis=0)
    y = _rms(o, lb_ref[3:4, :])
    o_ref[...] = (y * (og * _sigmoid(og))).astype(o_ref.dtype)


def hgrn_sample(c_hg, lbp, state, layer, batch, seq):
    width = c_hg.shape[1] // 4
    nh = width // HG_DK
    rows = CHUNK
    nseq = rows // seq
    nlev = int(math.log2(seq))
    wmat, masks = _level_constants(rows, seq)
    col = lambda j: pl.BlockSpec((rows, HG_DK), lambda i, h: (i, j * nh + h))
    lbspec = pl.BlockSpec((lbp.shape[0], HG_DK), lambda i, h: (0, h))
    return pl.pallas_call(
        functools.partial(_hgrn_sample_body, nlev=nlev, seq=seq),
        grid=(batch // nseq, nh),
        in_specs=[col(0), col(1), col(2), col(3), lbspec,
                  pl.BlockSpec(wmat.shape, lambda i, h: (0, 0)),
                  pl.BlockSpec(masks.shape, lambda i, h: (0, 0, 0)),
                  pl.BlockSpec((None, nseq, 1, HG_DK, HG_DK), lambda i, h: (layer, i, h, 0, 0))],
        out_specs=[pl.BlockSpec((rows, HG_DK), lambda i, h: (i, h)),
                   pl.BlockSpec((nseq, 1, HG_DK, HG_DK), lambda i, h: (i, h, 0, 0))],
        out_shape=[jax.ShapeDtypeStruct((batch * seq, width), BF16),
                   jax.ShapeDtypeStruct((batch, nh, HG_DK, HG_DK), F32)],
        compiler_params=_params("parallel", "parallel"),
        name="hgrn_sample",
    )(c_hg, c_hg, c_hg, c_hg, lbp, jnp.asarray(wmat, BF16), jnp.asarray(masks), state)


def _group_sum(x, bd):
    return _dg(jnp.concatenate(_split2(x), axis=1), bd, NN)


def _group_sum_wide(x, bd):
    return jnp.concatenate([_group_sum(x[:, j * LANES:(j + 1) * LANES], bd) for j in range(x.shape[1] // LANES)],
                           axis=1)


def _head_blockdiag(n, head):
    i = np.arange(n) // head
    bd = (i[:, None] == i[None, :]).astype(np.float32)
    return np.concatenate([bd, bd], axis=0)


def _rwkv_consts(rows, seq):
    t = np.arange(rows)
    sq = t // seq
    same = sq[:, None] == sq[None, :]
    r = t[None, :]
    pre = same & (r <= t[:, None])
    mid = sq * seq + seq // 2 - 1
    premid = same & (r <= mid[:, None])
    suf = same & (r > t[:, None])
    wmat = np.concatenate([pre.astype(np.float32) - premid.astype(np.float32), pre, suf], 0)
    t2 = np.concatenate([t, t])
    h2 = np.concatenate([0 * t, 0 * t + 1])
    s2 = t2 // seq
    blk = (h2[:, None] == h2[None, :]) & (s2[:, None] == s2[None, :])
    strict = blk & (t2[None, :] < t2[:, None])
    incl = blk & (t2[None, :] <= t2[:, None])
    lv = []
    for l in range(1, int(math.log2(seq)) + 1):
        m = 2 ** l
        par, pos = t2 // m, t2 % m
        lv.append(blk & (par[:, None] == par[None, :]) & (pos[:, None] >= m // 2) & (pos[None, :] < m // 2))
    masks = np.stack([strict, incl, np.eye(2 * rows, dtype=bool)] + lv).astype(np.float32)
    first = (t[:, None] == (np.arange(rows // seq) * seq)[None, :]).astype(np.float32)
    return np.concatenate([wmat] * 3, axis=1), masks, first


def _stack2(x, m0, m1):
    return jnp.concatenate([x * m0, x * m1], axis=0)


def _rwkv_tile(c_ref, cp_ref, sh_ref, mu_ref, vec_ref, wup_ref, aup_ref, gup_ref, bd_ref, w_ref, m_ref, f_ref,
               *, seq, first_chunk):
    rows = c_ref.shape[0]
    w = vec_ref.shape[1]
    npair = w // LANES
    nseq = rows // seq
    nlev = int(math.log2(seq))

    c = c_ref[...].astype(F32)
    rolled = pltpu.roll(c, 1, 0)
    rowi = lax.broadcasted_iota(jnp.int32, (rows, 1), 0)
    if nseq == 1:
        last = cp_ref.shape[0] - 1
        head_row = jnp.where(first_chunk, sh_ref[...], cp_ref[last:last + 1, :].astype(F32))
        prev = jnp.where(rowi == 0, head_row, rolled)
    else:
        prev = jnp.where(rowi % seq == 0, _dot_sel(f_ref[...], sh_ref[...]), rolled)
    xs = c + mu_ref[...] * (prev - c)

    rr, k, vv = xs[:, 0:w], xs[:, w:2 * w], xs[:, 2 * w:3 * w]
    lora = xs[:, 3 * w:3 * w + RW_LORA_W + RW_LORA_A]
    gl = xs[:, 3 * w + RW_LORA_W + RW_LORA_A:]
    w0, a0, k_k, k_a, r_k = (vec_ref[i:i + 1, :] for i in range(5))
    lw = -math.exp(-0.5) * _sigmoid(w0 + _dg(jnp.tanh(lora), wup_ref[...]))
    a = _sigmoid(a0 + _dg(lora, aup_ref[...]))
    gate = _dg(_sigmoid(gl), gup_ref[...])
    bd = bd_ref[...]
    kkr = k * k_k
    kk = kkr * lax.rsqrt(jnp.maximum(_group_sum_wide(kkr * kkr, bd), 1e-24))
    kx = k * (1.0 + (a - 1.0) * k_a)
    bonus = _group_sum_wide(rr * kx * r_k, bd) * vv
    bb = kk * a
    aa = -kk

    d = _dot_sel3(w_ref[...], lw)
    dm, gin, suf = d[0:rows], d[rows:2 * rows], d[2 * rows:3 * rows]
    e_inv = jnp.exp(-dm)
    e_abs = jnp.exp(gin)
    e_suf = jnp.exp(suf)
    a_mid = aa * jnp.exp(dm - lw)
    r_mid = rr * jnp.exp(dm)
    b_inv = bb * e_inv
    k_inv = kx * e_inv
    a_abs = aa * jnp.exp(gin - lw)
    r_abs = rr * e_abs
    b_end = bb * e_suf
    k_end = kx * e_suf

    lane = lax.broadcasted_iota(jnp.int32, (rows, LANES), 1)
    m0 = (lane < RW_HEAD).astype(F32)
    m1 = 1.0 - m0
    masks = [m_ref[i] for i in range(3 + nlev)]
    n2 = 2 * rows
    pairs = range(npair)
    s2 = lambda x, p: _stack2(x[:, p * LANES:(p + 1) * LANES], m0, m1)

    big = [_bdot(jnp.concatenate([s2(a_mid, p), s2(r_mid, p)], axis=0),
                 jnp.concatenate([s2(b_inv, p), s2(k_inv, p)], axis=0), NT) for p in pairs]
    nmat = [big[p][0:n2, 0:n2] * masks[0] for p in pairs]
    tinv = [masks[2] + nmat[p] * masks[3] for p in pairs]
    for l in range(2, nlev + 1):
        tinv = [tinv[p] + _bdot(_bdot(tinv[p], nmat[p] * masks[2 + l]), tinv[p]) for p in pairs]
    v_st = [s2(vv, p) for p in pairs]
    cmask = jnp.concatenate([masks[0], masks[1]], axis=0)
    kv = [_bdot(big[p][:, n2:] * cmask, v_st[p]) for p in pairs]
    pq = [_bdot(tinv[p], jnp.concatenate([s2(a_abs, p), kv[p][0:n2]], axis=1)) for p in pairs]
    ry = [_bdot(big[p][n2:, 0:n2] * masks[1], pq[p]) for p in pairs]
    eye2 = masks[2][0:RW_HEAD, 0:LANES] + masks[2][RW_HEAD:LANES, 0:LANES]
    r2s, y2s = [], []
    mxs = [[None] * npair for _ in range(nseq)]
    zs = [[None] * npair for _ in range(nseq)]
    for p in pairs:
        sl = slice(p * LANES, (p + 1) * LANES)
        r2 = s2(r_abs, p) + ry[p][:, 0:LANES]
        y2 = ry[p][:, LANES:] + kv[p][n2:]
        r2s.append(r2[0:rows] + r2[rows:])
        y2s.append(y2[0:rows] + y2[rows:])
        b_st = s2(b_end, p)
        k_st = s2(k_end, p)
        for b in range(nseq):
            pick = lambda x: jnp.concatenate([x[b * seq:(b + 1) * seq], x[rows + b * seq:rows + (b + 1) * seq]], axis=0)
            mz = _bdot(pick(pq[p]), pick(b_st), TN)
            zz = mz[LANES:] + _bdot(pick(v_st[p]), pick(k_st), TN)
            g_end = e_abs[b * seq + seq - 1:b * seq + seq, sl]
            mxs[b][p] = mz[0:RW_HEAD] + mz[RW_HEAD:LANES] + eye2 * g_end
            zs[b][p] = zz[0:RW_HEAD] + zz[RW_HEAD:]
    return gate, bonus, r2s, y2s, mxs, zs


def _rwkv_chunk_body(c_ref, cp_ref, sh_ref, mu_ref, vec_ref, wup_ref, aup_ref, gup_ref, bd_ref, w_ref, m_ref,
                     f_ref, r2_ref, y2_ref, bn_ref, g_ref, mx_ref, z_ref, *, seq, nchunk):
    gate, bonus, r2s, y2s, mxs, zs = _rwkv_tile(
        c_ref, cp_ref, sh_ref, mu_ref, vec_ref, wup_ref, aup_ref, gup_ref, bd_ref, w_ref, m_ref, f_ref,
        seq=seq, first_chunk=pl.program_id(0) % nchunk == 0)
    g_ref[...] = gate.astype(g_ref.dtype)
    bn_ref[...] = bonus.astype(bn_ref.dtype)
    for p in range(len(r2s)):
        sl = slice(p * LANES, (p + 1) * LANES)
        r2_ref[:, sl] = r2s[p].astype(r2_ref.dtype)
        y2_ref[:, sl] = y2s[p].astype(y2_ref.dtype)
        for b in range(len(mxs)):
            hs = slice(b * RW_HEAD, (b + 1) * RW_HEAD)
            mx_ref[hs, sl] = mxs[b][p].astype(mx_ref.dtype)
            z_ref[hs, sl] = zs[b][p].astype(z_ref.dtype)


def _rwkv_finish(y, bonus, gate, vec_ref, bd):
    mean = _group_sum_wide(y, bd) * (1.0 / RW_HEAD)
    yc = y - mean
    var = _group_sum_wide(yc * yc, bd) * (1.0 / RW_HEAD)
    yn = yc * lax.rsqrt(var + RW_LNX_EPS) * vec_ref[0:1, :] + vec_ref[1:2, :]
    return (yn + bonus) * gate


def _rwkv_fused_body(c_ref, cp_ref, sh_ref, mu_ref, vec_ref, wup_ref, aup_ref, gup_ref, bd_ref, w_ref, m_ref,
                     f_ref, s0_ref, ovec_ref, o_ref, s_ref, st_ref, *, seq):
    ci = pl.program_id(1)

    @pl.when(ci == 0)
    def _():
        st_ref[...] = s0_ref[...]

    gate, bonus, r2s, y2s, mxs, zs = _rwkv_tile(
        c_ref, cp_ref, sh_ref, mu_ref, vec_ref, wup_ref, aup_ref, gup_ref, bd_ref, w_ref, m_ref, f_ref,
        seq=seq, first_chunk=ci == 0)
    lane = lax.broadcasted_iota(jnp.int32, (RW_HEAD, LANES), 1)
    m0 = (lane < RW_HEAD).astype(F32)
    m1 = 1.0 - m0
    ys = []
    for p in range(len(r2s)):
        sl = slice(p * LANES, (p + 1) * LANES)
        s = st_ref[0, :, sl]
        ys.append(_bdot(r2s[p], _stack2(s, m0, m1), NT) + y2s[p])
        st_ref[0, :, sl] = _bdot(s, _stack2(mxs[0][p], m0, m1), NN) + zs[0][p]
    y = jnp.concatenate(ys, axis=1)
    o_ref[...] = _rwkv_finish(y, bonus, gate, ovec_ref, bd_ref[...]).astype(o_ref.dtype)

    @pl.when(ci == pl.num_programs(1) - 1)
    def _():
        s_ref[...] = st_ref[...]


def _rwkv_tile_consts(lp, ncol, cseq):
    w = lp['rw_w0'].shape[0]
    wmat, masks, first = _rwkv_consts(CHUNK, cseq)
    vecs = jnp.stack([lp['rw_w0'], lp['rw_a0'], lp['rw_k_k'], lp['rw_k_a'], lp['rw_r_k'].reshape(w)])
    wup = jnp.concatenate([lp['rw_w_up'], jnp.zeros_like(lp['rw_a_up'])], axis=0)
    aup = jnp.concatenate([jnp.zeros_like(lp['rw_w_up']), lp['rw_a_up']], axis=0)
    bd = jnp.asarray(_head_blockdiag(LANES, RW_HEAD), BF16)
    return [lp['rw_mu'].reshape(1, ncol), vecs, wup, aup, lp['rw_g_up'], bd, jnp.asarray(wmat, BF16),
            jnp.asarray(masks), jnp.asarray(first, BF16)]


def rwkv_fused(c_rw, shift0, s0, lp, batch, seq):
    rows_all, ncol = c_rw.shape
    w = lp['rw_w0'].shape[0]
    rows = CHUNK
    nchunk = seq // rows
    consts = _rwkv_tile_consts(lp, ncol, rows)
    ovecs = jnp.stack([lp['rw_lnx_w'], lp['rw_lnx_b']])
    full = lambda a: pl.BlockSpec(a.shape, lambda b, c: (0,) * a.ndim)
    pblk = SUBLANES * (4 // c_rw.dtype.itemsize)
    per = rows // pblk
    sblk = pl.BlockSpec((1, RW_HEAD, w), lambda b, c: (b, 0, 0))
    return pl.pallas_call(
        functools.partial(_rwkv_fused_body, seq=rows),
        grid=(batch, nchunk),
        in_specs=[pl.BlockSpec((rows, ncol), lambda b, c: (b * nchunk + c, 0)),
                  pl.BlockSpec((pblk, ncol), lambda b, c: (jnp.maximum((b * nchunk + c) * per - 1, 0), 0)),
                  pl.BlockSpec((None, 1, ncol), lambda b, c: (b, 0, 0))]
                 + [full(a) for a in consts] + [sblk, full(ovecs)],
        out_specs=[pl.BlockSpec((rows, w), lambda b, c: (b * nchunk + c, 0)), sblk],
        out_shape=[jax.ShapeDtypeStruct((rows_all, w), BF16), jax.ShapeDtypeStruct(s0.shape, F32)],
        scratch_shapes=[pltpu.VMEM((1, RW_HEAD, w), F32)],
        compiler_params=_params("parallel", "arbitrary"),
        name="rwkv_fused",
    )(c_rw, c_rw, shift0.reshape(batch, 1, ncol), *consts, s0, ovecs)


def rwkv_chunk(c_rw, shift0, lp, batch, seq):
    rows_all, ncol = c_rw.shape
    w = lp['rw_w0'].shape[0]
    rows = CHUNK
    cseq = min(seq, rows)
    nseq = rows // cseq
    nchunk = seq // cseq
    if nseq == 1:
        sh = shift0.reshape(batch, 1, ncol)
        sh_spec = pl.BlockSpec((None, 1, ncol), lambda i: (i // nchunk, 0, 0))
    else:
        sh = shift0
        sh_spec = pl.BlockSpec((nseq, ncol), lambda i: (i, 0))
    consts = _rwkv_tile_consts(lp, ncol, cseq)
    full = lambda a: pl.BlockSpec(a.shape, lambda i: (0,) * a.ndim)
    row = pl.BlockSpec((rows, w), lambda i: (i, 0))
    st = pl.BlockSpec((nseq * RW_HEAD, w), lambda i: (i, 0))
    nst = rows_all // cseq * RW_HEAD
    pblk = SUBLANES * (4 // c_rw.dtype.itemsize)
    per = rows // pblk
    return pl.pallas_call(
        functools.partial(_rwkv_chunk_body, seq=cseq, nchunk=nchunk),
        grid=(rows_all // rows,),
        in_specs=[pl.BlockSpec((rows, ncol), lambda i: (i, 0)),
                  pl.BlockSpec((pblk, ncol), lambda i: (jnp.maximum(i * per - 1, 0), 0)),
                  sh_spec] + [full(a) for a in consts],
        out_specs=[row, row, row, row, st, st],
        out_shape=[jax.ShapeDtypeStruct((rows_all, w), BF16)] * 4 + [jax.ShapeDtypeStruct((nst, w), BF16)] * 2,
        compiler_params=_params("parallel"),
        name="rwkv_chunk",
    )(c_rw, c_rw, sh, *consts)


def _rwkv_scan_body(r2_ref, y2_ref, mx_ref, z_ref, s0_ref, bn_ref, g_ref, vec_ref, bd_ref,
                    o_ref, s_ref, st_ref, *, seq):
    c = pl.program_id(1)
    rows = r2_ref.shape[0]
    nunit = rows // seq
    nstate = st_ref.shape[0]
    w = r2_ref.shape[1]

    @pl.when(c == 0)
    def _():
        st_ref[...] = s0_ref[...]

    lane = lax.broadcasted_iota(jnp.int32, (RW_HEAD, LANES), 1)
    m0 = (lane < RW_HEAD).astype(F32)
    m1 = 1.0 - m0
    ys = []
    for u in range(nunit):
        rs = slice(u * seq, (u + 1) * seq)
        hs = slice(u * RW_HEAD, (u + 1) * RW_HEAD)
        b = u if nstate > 1 else 0
        yb = []
        for p in range(w // LANES):
            sl = slice(p * LANES, (p + 1) * LANES)
            s = st_ref[b, :, sl]
            yb.append(_bdot(r2_ref[rs, sl], _stack2(s, m0, m1), NT) + y2_ref[rs, sl].astype(F32))
            st_ref[b, :, sl] = (_bdot(s, _stack2(mx_ref[hs, sl].astype(F32), m0, m1), NN)
                                + z_ref[hs, sl].astype(F32))
        ys.append(jnp.concatenate(yb, axis=1))
    y = jnp.concatenate(ys, axis=0) if nunit > 1 else ys[0]
    o_ref[...] = _rwkv_finish(y, bn_ref[...].astype(F32), g_ref[...].astype(F32), vec_ref,
                              bd_ref[...]).astype(o_ref.dtype)

    @pl.when(c == pl.num_programs(1) - 1)
    def _():
        s_ref[...] = st_ref[...]


def rwkv_scan(r2, y2, mx, z, s0, bonus, g, vecs, batch, seq):
    rows_all, w = r2.shape
    cseq = min(seq, CHUNK)
    if seq > CHUNK:
        rows, nseq = min(seq, 2 * CHUNK), 1
    else:
        rows, nseq = CHUNK, CHUNK // cseq
    nunit = rows // cseq
    nc = max(seq // rows, 1)
    bd = jnp.asarray(_head_blockdiag(LANES, RW_HEAD), BF16)
    row = pl.BlockSpec((rows, w), lambda i, c: (i * nc + c, 0))
    st = pl.BlockSpec((nunit * RW_HEAD, w), lambda i, c: (i * nc + c, 0))
    sblk = pl.BlockSpec((nseq, RW_HEAD, w), lambda i, c: (i, 0, 0))
    return pl.pallas_call(
        functools.partial(_rwkv_scan_body, seq=cseq),
        grid=(batch // nseq, nc),
        in_specs=[row, row, st, st, sblk, row, row,
                  pl.BlockSpec(vecs.shape, lambda i, c: (0, 0)), pl.BlockSpec(bd.shape, lambda i, c: (0, 0))],
        out_specs=[row, sblk],
        out_shape=[jax.ShapeDtypeStruct((rows_all, w), BF16), jax.ShapeDtypeStruct(s0.shape, F32)],
        scratch_shapes=[pltpu.VMEM((nseq, RW_HEAD, w), F32)],
        compiler_params=_params("parallel", "arbitrary"),
        name="rwkv_scan",
    )(r2, y2, mx, z, s0, bonus, g, vecs, bd)


def rwkv_branch(c_rw, shift0, s0, lp, batch, seq):
    ncol = c_rw.shape[1]
    w = lp['rw_w0'].shape[0]
    nh = w // RW_HEAD
    s0t = jnp.transpose(s0, (0, 2, 1, 3)).reshape(batch, RW_HEAD, w)
    if seq > CHUNK:
        out, s_fin = rwkv_fused(c_rw, shift0, s0t, lp, batch, seq)
    else:
        r2, y2, bonus, g, mx, z = rwkv_chunk(c_rw, shift0, lp, batch, seq)
        ovecs = jnp.stack([lp['rw_lnx_w'], lp['rw_lnx_b']])
        out, s_fin = rwkv_scan(r2, y2, mx, z, s0t, bonus, g, ovecs, batch, seq)
    s_fin = jnp.transpose(s_fin.reshape(batch, RW_HEAD, nh, RW_HEAD), (0, 2, 1, 3))
    return out, s_fin, c_rw.reshape(batch, seq, ncol)[:, -1].astype(F32)


def _att_prompt_body(q_ref, k_ref, v_ref, o_ref, l_ref, *, sub, dil):
    scale = ATT_HD ** -0.5
    nblk = q_ref.shape[0] // (sub * dil)
    qi = lax.broadcasted_iota(jnp.int32, (sub, 2 * sub), 0)
    ki = lax.broadcasted_iota(jnp.int32, (sub, 2 * sub), 1) - sub
    dist = qi - ki
    valid_two = (dist >= 0) & (dist <= sub)
    valid_one = valid_two & (ki >= 0)
    lane = lax.broadcasted_iota(jnp.int32, (1, LANES), 1)
    hm = [(lane < ATT_HD).astype(F32), (lane >= ATT_HD).astype(F32)]
    for r in range(dil):
        kprev = vprev = None
        for j in range(nblk):
            rs = pl.ds(r + j * sub * dil, sub, stride=dil) if dil > 1 else pl.ds(j * sub, sub)
            q = q_ref[rs, :] * scale
            kcur = k_ref[rs, :].astype(BF16)
            vcur = v_ref[rs, :].astype(BF16)
            vcur = jnp.concatenate([vcur, jnp.ones_like(vcur)], axis=1)
            if j == 0:
                kprev, vprev, valid = kcur, vcur, valid_one
            else:
                valid = valid_two
            kcat = jnp.concatenate([kprev, kcur], axis=0)
            vone = jnp.concatenate([vprev, vcur], axis=0)
            kprev, vprev = kcur, vcur
            o = jnp.zeros((sub, LANES), F32)
            lse = jnp.zeros((sub, LANES), F32)
            for m in hm:
                s = _dg((q * m).astype(BF16), kcat, NT)
                s = jnp.where(valid, s, -jnp.inf)
                mx = jnp.max(s, axis=-1, keepdims=True)
                pr = jnp.exp((s - mx).astype(BF16))
                od = _dg(pr, vone, NN)
                den = od[:, LANES:]
                o = o + od[:, 0:LANES] * (m / den)
                lse = lse + (mx + jnp.log(den)) * m
            o_ref[rs, :] = o
            l_ref[rs, :] = lse


def att_prompt_group(c_att, gi, batch, seq):
    win, dil = ATT_GROUPS[gi]
    sub = win // dil
    assert seq % win == 0
    ppw = ATT_WIDTH // LANES
    col = lambda j: pl.BlockSpec((seq, LANES), lambda b, p: (b, (3 * gi + j) * ppw + p))
    outb = pl.BlockSpec((seq, LANES), lambda b, p: (b, p))
    return pl.pallas_call(
        functools.partial(_att_prompt_body, sub=sub, dil=dil),
        grid=(batch, ppw),
        in_specs=[col(0), col(1), col(2)],
        out_specs=[outb, outb],
        out_shape=[jax.ShapeDtypeStruct((batch * seq, ATT_WIDTH), F32)] * 2,
        compiler_params=_params("parallel", "parallel"),
        name="att_prompt",
    )(c_att, c_att, c_att)


def _att_sample_body(q_ref, kn_ref, vn_ref, c_ref, bias_ref, o_ref, l_ref, *, dil, seq):
    hpq = MXU_DIM // ATT_HD
    trow = lax.broadcasted_iota(jnp.int32, (hpq * seq, seq), 0) % seq
    tcol = lax.broadcasted_iota(jnp.int32, (hpq * seq, seq), 1)
    new_ok = (tcol <= trow) & ((trow - tcol) % dil == 0)
    lane_head = lax.broadcasted_iota(jnp.int32, (seq, MXU_DIM), 1) // ATT_HD
    hm = [(lane_head == h).astype(F32) for h in range(hpq)]
    bias = bias_ref[...]
    for b in range(c_ref.shape[0]):
        rs = slice(b * seq, (b + 1) * seq)
        q = q_ref[rs, :] * (ATT_HD ** -0.5)
        kn, vn = kn_ref[rs, :], vn_ref[rs, :]
        for g in range(ATT_WIDTH // MXU_DIM):
            cs = slice(g * MXU_DIM, (g + 1) * MXU_DIM)
            qst = jnp.concatenate([q[:, cs] * m for m in hm], axis=0)
            sc = _bdot(qst, c_ref[b, 0, cs, :], NN) + bias
            sn = jnp.where(new_ok, _bdot(qst, kn[:, cs], NT), -jnp.inf)
            mx = jnp.maximum(jnp.max(sc, axis=-1, keepdims=True), jnp.max(sn, axis=-1, keepdims=True))
            pc = jnp.exp(sc - mx)
            pn = jnp.exp(sn - mx)
            den = jnp.sum(pc, axis=-1, keepdims=True) + jnp.sum(pn, axis=-1, keepdims=True)
            o = (_bdot(pc, c_ref[b, 1, cs, :], NT) + _bdot(pn, vn[:, cs], NN)) / den
            lse = mx + jnp.log(den)
            o_ref[rs, cs] = sum(o[h * seq:(h + 1) * seq] * hm[h] for h in range(hpq))
            l_ref[rs, cs] = sum(lse[h * seq:(h + 1) * seq] * hm[h] for h in range(hpq))


def att_sample_group(c_att, cache, layer, gi, batch, seq):
    win, dil = ATT_GROUPS[gi]
    depth, _, length = cache.shape[:3]
    assert length == win and seq <= win // dil
    cv = jnp.transpose(cache, (0, 1, 3, 4, 5, 2)).reshape(depth, batch, 2, ATT_WIDTH, length)
    hpq = MXU_DIM // ATT_HD
    t = np.tile(np.arange(seq), hpq)[:, None]
    j = np.arange(length)[None, :]
    bias = jnp.asarray(np.where((j >= t) & ((j - t) % dil == 0), 0.0, -np.inf), F32)
    nb = max(1, min(8, ATT_GROUPS[-1][0] // length))
    assert batch % nb == 0
    new = lambda c: pl.BlockSpec((nb * seq, ATT_WIDTH), lambda b: (b, 3 * gi + c))
    outb = pl.BlockSpec((nb * seq, ATT_WIDTH), lambda b: (b, 0))
    return pl.pallas_call(
        functools.partial(_att_sample_body, dil=dil, seq=seq),
        grid=(batch // nb,),
        in_specs=[new(0), new(1), new(2),
                  pl.BlockSpec((None, nb, 2, ATT_WIDTH, length), lambda b: (layer, b, 0, 0, 0)),
                  pl.BlockSpec(bias.shape, lambda b: (0, 0))],
        out_specs=[outb, outb],
        out_shape=[jax.ShapeDtypeStruct((batch * seq, ATT_WIDTH), F32)] * 2,
        compiler_params=_params("parallel"),
        name="att_sample",
    )(c_att, c_att, c_att, cv, bias)


def _merge_body(x_ref, gate_ref, hg_ref, rw_ref, o1_ref, o2_ref, o3_ref, l1_ref, l2_ref, l3_ref,
                whg_ref, wrw_ref, watt_ref, wout_ref, gn_ref, xo_ref, ho_ref):
    d = x_ref.shape[1]
    l1, l2, l3 = l1_ref[...], l2_ref[...], l3_ref[...]
    mx = jnp.maximum(jnp.maximum(l1, l2), l3)
    e1, e2, e3 = jnp.exp(l1 - mx), jnp.exp(l2 - mx), jnp.exp(l3 - mx)
    att = (e1 * o1_ref[...] + e2 * o2_ref[...] + e3 * o3_ref[...]) / (e1 + e2 + e3)
    gates = _sigmoid(gate_ref[...].astype(F32))
    bm = lambda a, w_ref: jnp.dot(a.astype(BF16), w_ref[...], preferred_element_type=F32)
    merged = (gates[:, 0:d] * bm(hg_ref[...], whg_ref)
              + gates[:, d:2 * d] * bm(rw_ref[...], wrw_ref)
              + gates[:, 2 * d:3 * d] * bm(att, watt_ref))
    xn = x_ref[...] + bm(merged, wout_ref)
    xo_ref[...] = xn
    ho_ref[...] = _rms(xn, gn_ref[...]).astype(ho_ref.dtype)


def merge_branches(x, c_gate, o_hg, o_rw, att_o, att_l, w_hg, w_rw, w_att, w_out, g_next):
    rows, d = x.shape
    tm = _row_tile(rows, 256)
    row = lambda a: pl.BlockSpec((tm, a.shape[1]), lambda i: (i, 0))
    full = lambda a: pl.BlockSpec(a.shape, lambda i: (0, 0))
    g2 = g_next.reshape(1, d)
    args = [x, c_gate, o_hg, o_rw, *att_o, *att_l]
    wts = [w_hg, w_rw, w_att, w_out, g2]
    return pl.pallas_call(
        _merge_body,
        grid=(rows // tm,),
        in_specs=[row(a) for a in args] + [full(a) for a in wts],
        out_specs=[pl.BlockSpec((tm, d), lambda i: (i, 0))] * 2,
        out_shape=[jax.ShapeDtypeStruct((rows, d), F32), jax.ShapeDtypeStruct((rows, d), BF16)],
        compiler_params=_params("parallel"),
        name="merge_branches",
    )(*args, *wts)


def _trunk_layer(x, h, lw, layer, batch, seq, state, last):
    d = x.shape[1]
    x, h = ffn_half_step(x, h, lw['ffn1_w_gu'], lw['ffn1_w_down'], lw['norm_mix'], BF16)
    c_hg = project(h, lw['w_in_hg'], 1024)
    c_rw = project(h, lw['w_in_rw'], lw['w_in_rw'].shape[1] // 2, BF16)
    c_att = project(h, lw['w_in_att'], 768)
    c_gate = project(h, lw['w_in_gate'], 1024, BF16)
    if state is None:
        o_hg, s_hg = hgrn_prompt(c_hg, lw['hg_params'], batch, seq)
        shift0 = jnp.zeros((batch, c_rw.shape[1]), F32)
        s_rw0 = jnp.zeros((batch, d // RW_HEAD, RW_HEAD, RW_HEAD), F32)
    else:
        s_hgrn, s_rwkv, s_shift, caches = state
        o_hg, s_hg = hgrn_sample(c_hg, lw['hg_params'], s_hgrn, layer, batch, seq)
        shift0, s_rw0 = s_shift[layer], s_rwkv[layer]
    o_rw, s_rw, shift = rwkv_branch(c_rw, shift0, s_rw0, lw, batch, seq)
    att_o, att_l, kv_rows = [], [], []
    c_att3 = c_att.reshape(batch, seq, -1)
    for gi, (win, dil) in enumerate(ATT_GROUPS):
        if state is None:
            o, l = att_prompt_group(c_att, gi, batch, seq)
        else:
            o, l = att_sample_group(c_att, caches[gi], layer, gi, batch, seq)
        att_o.append(o)
        att_l.append(l)
        keep = min(win, seq)
        kv = c_att3[:, seq - keep:, (3 * gi + 1) * ATT_WIDTH:(3 * gi + 3) * ATT_WIDTH]
        kv_rows.append(kv.reshape(batch, keep, 2, ATT_HEADS, ATT_HD))
    x, h = merge_branches(x, c_gate, o_hg, o_rw, att_o, att_l, lw['w_branch_hg'], lw['w_branch_rw'],
                          lw['w_branch_att'], lw['w_out'], lw['norm_ffn2'])
    x, h = ffn_half_step(x, h, lw['ffn2_w_gu'], lw['ffn2_w_down'], lw['norm_next'], F32 if last else BF16)
    return x, h, (s_hg, s_rw, shift, *kv_rows)


def _lower_bounds(hg_lb):
    p = jax.nn.softmax(hg_lb.astype(F32), axis=0)
    c = jnp.cumsum(p, axis=0)
    return c - c[0:1]


def kernel(x_prompt, x_sample, state_hgrn, state_rwkv, state_rwkv_shift, cache_att1_kv, cache_att2_kv, cache_att3_kv, norm_ffn1, ffn1_w_gu, ffn1_w_down, norm_mix, w_in, hg_lb, hg_gnorm, rw_mu, rw_w0, rw_w_up, rw_a0, rw_a_up, rw_g_up, rw_k_k, rw_k_a, rw_r_k, rw_lnx_w, rw_lnx_b, w_branch_hg, w_branch_rw, w_branch_att, w_out, norm_ffn2, ffn2_w_gu, ffn2_w_down, norm_final):
    depth = norm_ffn1.shape[0]
    bp, sp, d = x_prompt.shape
    bs, ss, _ = x_sample.shape
    hg_cols = 2 * HG_HEADS * HG_DK + 2 * hg_gnorm.shape[1]
    rw_cols = rw_mu.shape[1]
    att_cols = 3 * len(ATT_GROUPS) * ATT_WIDTH
    bounds = np.cumsum([0, hg_cols, rw_cols, att_cols, 3 * d])
    assert bounds[-1] == w_in.shape[2]
    lbs = _lower_bounds(hg_lb)
    layers = []
    for l in range(depth):
        lb = lbs[l]
        w_cols = lambda i: w_in[l, :, bounds[i]:bounds[i + 1]].astype(BF16)
        layers.append({
            'ffn1_w_gu': ffn1_w_gu[l].astype(BF16), 'ffn1_w_down': ffn1_w_down[l].astype(BF16),
            'ffn2_w_gu': ffn2_w_gu[l].astype(BF16), 'ffn2_w_down': ffn2_w_down[l].astype(BF16),
            'norm_mix': norm_mix[l], 'norm_ffn2': norm_ffn2[l],
            'norm_next': norm_ffn1[l + 1] if l + 1 < depth else norm_final,
            'w_in_hg': w_cols(0), 'w_in_rw': w_cols(1), 'w_in_att': w_cols(2), 'w_in_gate': w_cols(3),
            'hg_params': jnp.stack([jnp.log(lb), jnp.log1p(-lb), 1.0 - lb, hg_gnorm[l]]),
            'rw_mu': rw_mu[l], 'rw_w0': rw_w0[l], 'rw_w_up': rw_w_up[l], 'rw_a0': rw_a0[l], 'rw_a_up': rw_a_up[l],
            'rw_g_up': rw_g_up[l], 'rw_k_k': rw_k_k[l], 'rw_k_a': rw_k_a[l], 'rw_r_k': rw_r_k[l],
            'rw_lnx_w': rw_lnx_w[l], 'rw_lnx_b': rw_lnx_b[l],
            'w_branch_hg': w_branch_hg[l].astype(BF16), 'w_branch_rw': w_branch_rw[l].astype(BF16),
            'w_branch_att': w_branch_att[l].astype(BF16), 'w_out': w_out[l].astype(BF16),
        })
    caches = (cache_att1_kv, cache_att2_kv, cache_att3_kv)
    sample_state = (state_hgrn, state_rwkv, state_rwkv_shift, caches)
    results = []
    for x3, batch, seq, state in ((x_prompt, bp, sp, None), (x_sample, bs, ss, sample_state)):
        x = x3.reshape(batch * seq, d)
        h = rmsnorm_rows(x, norm_ffn1[0], BF16)
        states = []
        for l in range(depth):
            x, h, st = _trunk_layer(x, h, layers[l], l, batch, seq, state, l + 1 == depth)
            states.append(st)
        stacked = [jnp.stack(z, axis=0) for z in zip(*states)]
        results.append((h.reshape(batch, seq, d), stacked))
    (yp, pst), (ys, sst) = results
    return (yp, ys, *pst, *sst)
```
